```python
import jax
import jax.numpy as jnp
from jax import lax
import numpy as np

D_MODEL = 1024
BATCH = 4
SEQ = 4096
DEPTH = 4
DEC_BATCH = 128
DEC_SEQ = 1
PAST_LEN = 8192
PAGE_SIZE = 128

N_META = 16
N_EVEN = (DEPTH + 1) // 2
N_ODD = DEPTH // 2
EPS = 1e-6
MLA_HEADS = 8
Q_LORA = 384
KV_LORA = 256
QK_NOPE = 64
QK_ROPE = 32
V_HEAD = 64
QK_HEAD = QK_NOPE + QK_ROPE
LAT_W = KV_LORA + QK_ROPE
ROPE_THETA = 10000.0
Q_BLOCK = 128
ML_HEADS = 4
ML_HEAD = 128
ML_W = ML_HEADS * ML_HEAD
ML_CHUNK = 64
CONV_W = 4
LRU_W = D_MODEL
LRU_BLOCKS = 8
LRU_BLOCK = LRU_W // LRU_BLOCKS
LRU_C = 8.0
D_FF = 2816
N_EXPERTS = 8
TOP_K = 2
D_FF_EXPERT = 3584
IN_EVEN = Q_LORA + KV_LORA + QK_ROPE + 3 * ML_W + 2 * ML_HEADS
MIX_EVEN = MLA_HEADS * V_HEAD + ML_W

kernel_name = 'hybrid_mla_mlstm_rglru_moe_step'


def rmsnorm(x, g):
    xf = x.astype(jnp.float32)
    y = xf * lax.rsqrt(jnp.mean(xf * xf, axis=-1, keepdims=True) + EPS)
    return (y * g.astype(jnp.float32)).astype(x.dtype)


def rope(x, pos):
    half = QK_ROPE // 2
    inv_freq = ROPE_THETA ** (-jnp.arange(half, dtype=jnp.float32) / half)
    ang = pos.astype(jnp.float32)[:, None] * inv_freq[None, :]
    cos = jnp.cos(ang)[:, None, :]
    sin = jnp.sin(ang)[:, None, :]
    xf = x.astype(jnp.float32)
    x1, x2 = xf[..., :half], xf[..., half:]
    return jnp.concatenate([x1 * cos - x2 * sin, x1 * sin + x2 * cos], axis=-1).astype(x.dtype)


def causal_conv(u, buf, w, b):
    T = u.shape[1]
    ext = jnp.concatenate([buf.astype(u.dtype), u], axis=1)
    out = b
    for j in range(CONV_W):
        out = out + ext[:, j:j + T] * w[j]
    return out, ext[:, -(CONV_W - 1):]


def swiglu(h, w1, w3, w2):
    return (jax.nn.silu(h @ w1) * (h @ w3)) @ w2


def moe_swiglu(h, w_router, w1, w3, w2):
    B, T, D = h.shape
    hf = h.reshape(B * T, D)
    logits = (hf @ w_router).astype(jnp.float32)
    top_v, top_i = lax.top_k(logits, TOP_K)
    gates = jax.nn.softmax(top_v, axis=-1)
    combine = jnp.sum(jax.nn.one_hot(top_i, N_EXPERTS, dtype=jnp.float32) * gates[..., None], axis=1).astype(h.dtype)
    out = jnp.zeros_like(hf)
    for e in range(N_EXPERTS):
        out = out + combine[:, e:e + 1] * swiglu(hf, w1[e], w3[e], w2[e])
    return out.reshape(B, T, D)


def mla_keys(rows, w_uk, k_norm_g):
    k_nope = jnp.einsum('btc,chd->bthd', rows[..., :KV_LORA], w_uk)
    k_rope = jnp.broadcast_to(rows[:, :, None, KV_LORA:], k_nope.shape[:3] + (QK_ROPE,))
    return rmsnorm(jnp.concatenate([k_nope, k_rope], axis=-1), k_norm_g)


def attend(q, q_pos, k, lat, k_pos):
    s = jnp.einsum('bqhd,bkhd->bhqk', q, k).astype(jnp.float32) * (QK_HEAD ** -0.5)
    s = jnp.where(k_pos[None, :] <= q_pos[:, None], s, -1e30)
    p = jax.nn.softmax(s, axis=-1).astype(lat.dtype)
    return jnp.einsum('bhqk,bkc->bqhc', p, lat)


def prompt_attn(w_uk, k_norm_g):
    def fn(q, rows):
        B, T = q.shape[:2]
        pos = jnp.arange(T, dtype=jnp.int32)
        k = mla_keys(rows, w_uk, k_norm_g)
        lat = rows[..., :KV_LORA]
        o_meta = attend(q[:, :N_META], pos[:N_META], k[:, :N_META], lat[:, :N_META], pos[:N_META])
        nb = (T - N_META) // Q_BLOCK
        qb = jnp.moveaxis(q[:, N_META:].reshape(B, nb, Q_BLOCK, MLA_HEADS, QK_HEAD), 1, 0)
        pb = pos[N_META:].reshape(nb, Q_BLOCK)
        ob = lax.map(lambda a: attend(a[0], a[1], k, lat, pos), (qb, pb))
        o_real = jnp.moveaxis(ob, 0, 1).reshape(B, T - N_META, MLA_HEADS, KV_LORA)
        return jnp.concatenate([o_meta, o_real], axis=1)
    return fn


def sample_attn(cache_l, page_table, past_len, w_uk, k_norm_g):
    def fn(q, rows):
        Tq = q.shape[1]
        q_pos = past_len + jnp.arange(Tq, dtype=jnp.int32)
        k_pos = jnp.arange(past_len + Tq, dtype=jnp.int32)

        def one(args):
            pt, qb, rb = args
            past = cache_l[pt].reshape(-1, LAT_W)
            r_all = jnp.concatenate([past, rb.astype(cache_l.dtype)], axis=0)[None]
            k = mla_keys(r_all, w_uk, k_norm_g)
            return attend(qb[None], q_pos, k, r_all[..., :KV_LORA], k_pos)[0]
        return lax.map(one, (page_table, q, rows))
    return fn


def mlstm_chunk(state, inp):
    C, n, m = state
    q, k, v, li, lf = inp
    L = q.shape[2]
    b = jnp.cumsum(lf, axis=-1)
    causal = jnp.tril(jnp.ones((L, L), dtype=bool))
    dmat = jnp.where(causal, b[..., :, None] - b[..., None, :] + li[..., None, :], -jnp.inf)
    inter = b + m[..., None]
    m_t = jnp.maximum(inter, jnp.max(dmat, axis=-1))
    w_intra = jnp.exp(dmat - m_t[..., None])
    w_inter = jnp.exp(inter - m_t)
    s = jnp.einsum('bhtd,bhsd->bhts', q, k) * w_intra
    num = jnp.einsum('bhts,bhsv->bhtv', s, v) + w_inter[..., None] * jnp.einsum('bhvd,bhtd->bhtv', C, q)
    den = jnp.sum(s, axis=-1) + w_inter * jnp.einsum('bhd,bhtd->bht', n, q)
    h = num / jnp.maximum(jnp.abs(den), jnp.exp(-m_t))[..., None]
    b_end = b[..., -1]
    g = b_end[..., None] - b + li
    m_new = jnp.maximum(b_end + m, jnp.max(g, axis=-1))
    wg = jnp.exp(g - m_new[..., None])
    wc = jnp.exp(b_end + m - m_new)
    C_new = wc[..., None, None] * C + jnp.einsum('bhs,bhsv,bhsd->bhvd', wg, v, k)
    n_new = wc[..., None] * n + jnp.einsum('bhs,bhsd->bhd', wg, k)
    return (C_new, n_new, m_new), h


def mlstm_prompt_cell(state, q, k, v, li, lf):
    head = lambda a: a[:, :, :N_META]
    state, h_meta = mlstm_chunk(state, (head(q), head(k), head(v), head(li), head(lf)))
    n_real = q.shape[2] - N_META
    n_chunks = n_real // ML_CHUNK

    def to_chunks(a):
        a = a[:, :, N_META:]
        a = a.reshape(a.shape[:2] + (n_chunks, ML_CHUNK) + a.shape[3:])
        return jnp.moveaxis(a, 2, 0)
    state, h_real = lax.scan(mlstm_chunk, state, (to_chunks(q), to_chunks(k), to_chunks(v), to_chunks(li), to_chunks(lf)))
    h_real = jnp.moveaxis(h_real, 0, 2)
    h_real = h_real.reshape(h_real.shape[:2] + (n_real, ML_HEAD))
    return state, jnp.concatenate([h_meta, h_real], axis=2)


def mlstm_sample_cell(state, q, k, v, li, lf):
    return mlstm_chunk(state, (q, k, v, li, lf))


def even_mixer(h, pos, attn_fn, ml_cell, ml_state, ml_buf, p):
    B, T, _ = h.shape
    z = h @ p['w_in']
    sizes = (Q_LORA, KV_LORA, QK_ROPE, ML_W, ML_W, ML_W, ML_HEADS, ML_HEADS)
    idx = [int(i) for i in np.cumsum(sizes)[:-1]]
    cq, ckv, kr, u, v, og, ig, fg = jnp.split(z, idx, axis=-1)
    q = (rmsnorm(cq, p['q_a_g']) @ p['w_uq']).reshape(B, T, MLA_HEADS, QK_HEAD)
    q = jnp.concatenate([q[..., :QK_NOPE], rope(q[..., QK_NOPE:], pos)], axis=-1)
    q = rmsnorm(q, p['q_norm_g'])
    rows = jnp.concatenate([rmsnorm(ckv, p['kv_a_g']), rope(kr[:, :, None, :], pos)[:, :, 0]], axis=-1)
    o_lat = attn_fn(q, rows)
    att = jnp.einsum('bthc,chv->bthv', o_lat, p['w_uv']).reshape(B, T, MLA_HEADS * V_HEAD)
    uc, ml_buf_new = causal_conv(u, ml_buf, p['ml_conv_w'], p['ml_conv_b'])
    uc = jax.nn.silu(uc).reshape(B, T, ML_HEADS, ML_HEAD)
    qm = jnp.einsum('bthi,hij->bhtj', uc, p['ml_w_q']).astype(jnp.float32)
    km = jnp.einsum('bthi,hij->bhtj', uc, p['ml_w_k']).astype(jnp.float32) * (ML_HEAD ** -0.5)
    vm = v.reshape(B, T, ML_HEADS, ML_HEAD).transpose(0, 2, 1, 3).astype(jnp.float32)
    li = (ig + p['ml_b_i']).astype(jnp.float32).transpose(0, 2, 1)
    lf = jax.nn.log_sigmoid((fg + p['ml_b_f']).astype(jnp.float32)).transpose(0, 2, 1)
    ml_state_new, hm = ml_cell(ml_state, qm, km, vm, li, lf)
    hm = rmsnorm(hm.transpose(0, 2, 1, 3), p['ml_norm_g']).astype(h.dtype)
    hm = hm.reshape(B, T, ML_W) * jax.nn.sigmoid(og)
    out = jnp.concatenate([att, hm], axis=-1) @ p['w_out']
    return out, rows, ml_state_new, ml_buf_new


def lru_combine(l, r):
    return (l[0] * r[0], r[0] * l[1] + r[1])


def odd_mixer(h, conv_buf, h0, p):
    B, T, _ = h.shape
    gate, xb = jnp.split(h @ p['w_in'], 2, axis=-1)
    xc, buf_new = causal_conv(xb, conv_buf, p['conv_w'], p['conv_b'])
    xg = xc.reshape(B, T, LRU_BLOCKS, LRU_BLOCK)
    r = jax.nn.sigmoid((jnp.einsum('btgi,gij->btgj', xg, p['w_a']).reshape(B, T, LRU_W) + p['b_a']).astype(jnp.float32))
    i = jax.nn.sigmoid((jnp.einsum('btgi,gij->btgj', xg, p['w_x']).reshape(B, T, LRU_W) + p['b_x']).astype(jnp.float32))
    log_a = -LRU_C * r * jax.nn.softplus(-p['lam'].astype(jnp.float32))
    a = jnp.exp(log_a)
    bx = jnp.sqrt(-jnp.expm1(2.0 * log_a)) * i * xc.astype(jnp.float32)
    bx = bx.at[:, 0].add(a[:, 0] * h0.astype(jnp.float32))
    _, hs = lax.associative_scan(lru_combine, (a, bx), axis=1)
    y = (hs * jax.nn.gelu(gate.astype(jnp.float32))).astype(h.dtype)
    return y @ p['w_out'], buf_new, hs[:, -1]


def setup_inputs(seed: int = 0) -> dict:
    key = jax.random.key(seed)
    ks = list(jax.random.split(key, 64))
    cnt = [0]

    def nk():
        cnt[0] += 1
        return ks[cnt[0] - 1]

    def nrm(shape, scale):
        return jax.random.normal(nk(), shape, jnp.float32) * scale

    def gain(shape):
        return 1.0 + nrm(shape, 0.02)

    n_pages = PAST_LEN // PAGE_SIZE
    n_pool = (5 * DEC_BATCH * n_pages) // 4
    page_table = jax.random.permutation(nk(), n_pool)[:DEC_BATCH * n_pages].reshape(DEC_BATCH, n_pages).astype(jnp.int32)
    u = jax.random.uniform(nk(), (N_ODD, LRU_W), jnp.float32, minval=0.9, maxval=0.999)
    a0 = u ** (1.0 / LRU_C)
    lru_lam = jnp.log(a0) - jnp.log1p(-a0)
    return {
        'x_prompt': nrm((BATCH, SEQ, D_MODEL), 1.0),
        'x_sample': nrm((DEC_BATCH, DEC_SEQ, D_MODEL), 1.0),
        'cache_mla': nrm((N_EVEN, n_pool, PAGE_SIZE, LAT_W), 1.0),
        'state_mlstm_C': nrm((N_EVEN, DEC_BATCH, ML_HEADS, ML_HEAD, ML_HEAD), 0.1),
        'state_mlstm_n': nrm((N_EVEN, DEC_BATCH, ML_HEADS, ML_HEAD), 0.1),
        'state_mlstm_m': nrm((N_EVEN, DEC_BATCH, ML_HEADS), 1.0),
        'state_mlstm_conv': nrm((N_EVEN, DEC_BATCH, CONV_W - 1, ML_W), 1.0),
        'state_rglru_h': nrm((N_ODD, DEC_BATCH, LRU_W), 0.5),
        'state_rglru_conv': nrm((N_ODD, DEC_BATCH, CONV_W - 1, LRU_W), 1.0),
        'page_table': page_table,
        'meta_tokens': nrm((N_META, D_MODEL), 1.0),
        'norm_mix_even': gain((N_EVEN, D_MODEL)),
        'w_in_even': nrm((N_EVEN, D_MODEL, IN_EVEN), D_MODEL ** -0.5),
        'mla_q_a_g': gain((N_EVEN, Q_LORA)),
        'mla_w_uq': nrm((N_EVEN, Q_LORA, MLA_HEADS * QK_HEAD), Q_LORA ** -0.5),
        'mla_kv_a_g': gain((N_EVEN, KV_LORA)),
        'mla_w_uk': nrm((N_EVEN, KV_LORA, MLA_HEADS, QK_NOPE), KV_LORA ** -0.5),
        'mla_w_uv': nrm((N_EVEN, KV_LORA, MLA_HEADS, V_HEAD), KV_LORA ** -0.5),
        'mla_q_norm_g': gain((N_EVEN, QK_HEAD)),
        'mla_k_norm_g': gain((N_EVEN, QK_HEAD)),
        'ml_conv_w': nrm((N_EVEN, CONV_W, ML_W), CONV_W ** -0.5),
        'ml_conv_b': nrm((N_EVEN, ML_W), 0.02),
        'ml_w_q': nrm((N_EVEN, ML_HEADS, ML_HEAD, ML_HEAD), ML_HEAD ** -0.5),
        'ml_w_k': nrm((N_EVEN, ML_HEADS, ML_HEAD, ML_HEAD), ML_HEAD ** -0.5),
        'ml_b_i': nrm((N_EVEN, ML_HEADS), 0.1),
        'ml_b_f': 3.0 + nrm((N_EVEN, ML_HEADS), 0.1),
        'ml_norm_g': gain((N_EVEN, ML_HEADS, ML_HEAD)),
        'w_out_even': nrm((N_EVEN, MIX_EVEN, D_MODEL), 0.5 * MIX_EVEN ** -0.5),
        'norm_ffn_even': gain((N_EVEN, D_MODEL)),
        'ffn_w1': nrm((N_EVEN, D_MODEL, D_FF), D_MODEL ** -0.5),
        'ffn_w3': nrm((N_EVEN, D_MODEL, D_FF), D_MODEL ** -0.5),
        'ffn_w2': nrm((N_EVEN, D_FF, D_MODEL), 0.5 * D_FF ** -0.5),
        'norm_mix_odd': gain((N_ODD, D_MODEL)),
        'w_in_odd': nrm((N_ODD, D_MODEL, 2 * LRU_W), D_MODEL ** -0.5),
        'lru_conv_w': nrm((N_ODD, CONV_W, LRU_W), CONV_W ** -0.5),
        'lru_conv_b': nrm((N_ODD, LRU_W), 0.02),
        'lru_w_a': nrm((N_ODD, LRU_BLOCKS, LRU_BLOCK, LRU_BLOCK), LRU_BLOCK ** -0.5),
        'lru_b_a': nrm((N_ODD, LRU_W), 0.1),
        'lru_w_x': nrm((N_ODD, LRU_BLOCKS, LRU_BLOCK, LRU_BLOCK), LRU_BLOCK ** -0.5),
        'lru_b_x': nrm((N_ODD, LRU_W), 0.1),
        'lru_lam': lru_lam,
        'w_out_odd': nrm((N_ODD, LRU_W, D_MODEL), 0.5 * LRU_W ** -0.5),
        'norm_ffn_odd': gain((N_ODD, D_MODEL)),
        'moe_router': nrm((N_ODD, D_MODEL, N_EXPERTS), D_MODEL ** -0.5),
        'moe_w1': nrm((N_ODD, N_EXPERTS, D_MODEL, D_FF_EXPERT), D_MODEL ** -0.5),
        'moe_w3': nrm((N_ODD, N_EXPERTS, D_MODEL, D_FF_EXPERT), D_MODEL ** -0.5),
        'moe_w2': nrm((N_ODD, N_EXPERTS, D_FF_EXPERT, D_MODEL), 0.5 * D_FF_EXPERT ** -0.5),
    }


def reference(x_prompt, x_sample, cache_mla, state_mlstm_C, state_mlstm_n, state_mlstm_m, state_mlstm_conv,
              state_rglru_h, state_rglru_conv, page_table, meta_tokens,
              norm_mix_even, w_in_even, mla_q_a_g, mla_w_uq, mla_kv_a_g, mla_w_uk, mla_w_uv, mla_q_norm_g,
              mla_k_norm_g, ml_conv_w, ml_conv_b, ml_w_q, ml_w_k, ml_b_i, ml_b_f, ml_norm_g, w_out_even,
              norm_ffn_even, ffn_w1, ffn_w3, ffn_w2,
              norm_mix_odd, w_in_odd, lru_conv_w, lru_conv_b, lru_w_a, lru_b_a, lru_w_x, lru_b_x, lru_lam,
              w_out_odd, norm_ffn_odd, moe_router, moe_w1, moe_w3, moe_w2):
    B = x_prompt.shape[0]
    meta = jnp.broadcast_to(meta_tokens[None].astype(x_prompt.dtype), (B, N_META, D_MODEL))
    xp = jnp.concatenate([meta, x_prompt], axis=1)
    xs = x_sample
    Tp = xp.shape[1]
    Ts = xs.shape[1]
    past_len = page_table.shape[1] * PAGE_SIZE
    pos_p = jnp.arange(Tp, dtype=jnp.int32)
    pos_s = past_len + jnp.arange(Ts, dtype=jnp.int32)
    f32 = jnp.float32
    rows_p, rows_s, Cp, Cs, np_, ns, mp, ms, cbp, cbs = [], [], [], [], [], [], [], [], [], []
    hp_l, hs_l, lbp, lbs = [], [], [], []
    for layer in range(DEPTH):
        if layer % 2 == 0:
            e = layer // 2
            p = {'w_in': w_in_even[e], 'q_a_g': mla_q_a_g[e], 'w_uq': mla_w_uq[e], 'kv_a_g': mla_kv_a_g[e],
                 'w_uv': mla_w_uv[e], 'q_norm_g': mla_q_norm_g[e], 'ml_conv_w': ml_conv_w[e],
                 'ml_conv_b': ml_conv_b[e], 'ml_w_q': ml_w_q[e], 'ml_w_k': ml_w_k[e], 'ml_b_i': ml_b_i[e],
                 'ml_b_f': ml_b_f[e], 'ml_norm_g': ml_norm_g[e], 'w_out': w_out_even[e]}
            st_p = (jnp.zeros((B, ML_HEADS, ML_HEAD, ML_HEAD), f32), jnp.zeros((B, ML_HEADS, ML_HEAD), f32),
                    jnp.zeros((B, ML_HEADS), f32))
            buf_p = jnp.zeros((B, CONV_W - 1, ML_W), xp.dtype)
            st_s = (state_mlstm_C[e].astype(f32), state_mlstm_n[e].astype(f32), state_mlstm_m[e].astype(f32))
            mix_p, r_p, stp, bp = even_mixer(rmsnorm(xp, norm_mix_even[e]), pos_p,
                                             prompt_attn(mla_w_uk[e], mla_k_norm_g[e]),
                                             mlstm_prompt_cell, st_p, buf_p, p)
            mix_s, r_s, sts, bs = even_mixer(rmsnorm(xs, norm_mix_even[e]), pos_s,
                                             sample_attn(cache_mla[e], page_table, past_len, mla_w_uk[e], mla_k_norm_g[e]),
                                             mlstm_sample_cell, st_s, state_mlstm_conv[e], p)
            xp = xp + mix_p
            xs = xs + mix_s
            rows_p.append(r_p); rows_s.append(r_s)
            Cp.append(stp[0]); np_.append(stp[1]); mp.append(stp[2])
            Cs.append(sts[0]); ns.append(sts[1]); ms.append(sts[2])
            cbp.append(bp); cbs.append(bs)
            xp = xp + swiglu(rmsnorm(xp, norm_ffn_even[e]), ffn_w1[e], ffn_w3[e], ffn_w2[e])
            xs = xs + swiglu(rmsnorm(xs, norm_ffn_even[e]), ffn_w1[e], ffn_w3[e], ffn_w2[e])
        else:
            o = layer // 2
            p = {'w_in': w_in_odd[o], 'conv_w': lru_conv_w[o], 'conv_b': lru_conv_b[o], 'w_a': lru_w_a[o],
                 'b_a': lru_b_a[o], 'w_x': lru_w_x[o], 'b_x': lru_b_x[o], 'lam': lru_lam[o], 'w_out': w_out_odd[o]}
            mix_p, bp, hp = odd_mixer(rmsnorm(xp, norm_mix_odd[o]), jnp.zeros((B, CONV_W - 1, LRU_W), xp.dtype),
                                      jnp.zeros((B, LRU_W), f32), p)
            mix_s, bs, hs = odd_mixer(rmsnorm(xs, norm_mix_odd[o]), state_rglru_conv[o], state_rglru_h[o], p)
            xp = xp + mix_p
            xs = xs + mix_s
            hp_l.append(hp); hs_l.append(hs); lbp.append(bp); lbs.append(bs)
            xp = xp + moe_swiglu(rmsnorm(xp, norm_ffn_odd[o]), moe_router[o], moe_w1[o], moe_w3[o], moe_w2[o])
            xs = xs + moe_swiglu(rmsnorm(xs, norm_ffn_odd[o]), moe_router[o], moe_w1[o], moe_w3[o], moe_w2[o])
    y_prompt = xp[:, N_META:]
    y_sample = xs
    return (y_prompt, y_sample,
            jnp.stack(rows_p), jnp.stack(rows_s),
            jnp.stack(Cp), jnp.stack(Cs), jnp.stack(np_), jnp.stack(ns), jnp.stack(mp), jnp.stack(ms),
            jnp.stack(cbp), jnp.stack(cbs),
            jnp.stack(hp_l), jnp.stack(hs_l), jnp.stack(lbp), jnp.stack(lbs))
```

```python
import functools

import jax
import jax.numpy as jnp
from jax import lax
from jax.experimental import pallas as pl
from jax.experimental.pallas import tpu as pltpu

F32 = jnp.float32
BF16 = jnp.bfloat16

N_META = 16
EPS = 1e-6
MLA_HEADS = 8
Q_LORA = 384
KV_LORA = 256
QK_NOPE = 64
QK_ROPE = 32
V_HEAD = 64
QK_HEAD = QK_NOPE + QK_ROPE
LAT_W = KV_LORA + QK_ROPE
ROPE_THETA = 10000.0
ML_HEADS = 4
ML_HEAD = 128
ML_W = ML_HEADS * ML_HEAD
CONV_W = 4
LRU_BLOCKS = 8
LRU_C = 8.0
TOP_K = 2
PAGE = 128

LANES = 128
SUBLANES = 8
ROWS16 = 16
HEAD_PAD = 128
SEQ_BLOCK = 128
NEG = -1e30
VMEM_LIMIT = 56 * 1024 * 1024

Z_CQ, Z_KR, Z_U, Z_V, Z_OG, Z_CKV, Z_GT, Z_W = 0, 384, 512, 1024, 1536, 2048, 2304, 2560


def _cparams(n_axes):
    return pltpu.CompilerParams(dimension_semantics=("arbitrary",) * n_axes,
                                vmem_limit_bytes=VMEM_LIMIT)


def _round_up(a, b):
    return (a + b - 1) // b * b


def _pick_tile(n, prefs):
    for t in prefs:
        if n % t == 0:
            return t
    if n < min(prefs):
        return n
    raise ValueError(f"no tile in {prefs} divides {n}")


def _sigmoid(x):
    return 1.0 / (1.0 + jnp.exp(-x))


def _log_sigmoid(x):
    return jnp.minimum(x, 0.0) - jnp.log(1.0 + jnp.exp(-jnp.abs(x)))


def _dot(a, b):
    return jnp.dot(a, b, preferred_element_type=F32)


def _dot_nt(a, b):
    return lax.dot_general(a, b, (((1,), (1,)), ((), ())), preferred_element_type=F32)


def _dot_tn(a, b):
    return lax.dot_general(a, b, (((0,), (0,)), ((), ())), preferred_element_type=F32)


def _split_bf16(x):
    hi = x.astype(BF16)
    lo = (x - hi.astype(F32)).astype(BF16)
    return hi, lo


def _dot3(a, b, dot=_dot):
    ah, al = _split_bf16(a)
    bh, bl = _split_bf16(b)
    return dot(ah, bh) + dot(al, bh) + dot(ah, bl)


def _mm(a, w, precise):
    if precise:
        return _dot3(a.astype(F32), w)
    return _dot(a.astype(BF16), w)


def _shift_rows(x, prev8, k):
    rolled = pltpu.roll(x, k, 0)
    head = pltpu.roll(prev8, k, 0)
    rid = lax.broadcasted_iota(jnp.int32, (SUBLANES, x.shape[1]), 0)
    top = jnp.where(rid < k, head, rolled[:SUBLANES])
    if x.shape[0] == SUBLANES:
        return top
    return jnp.concatenate([top, rolled[SUBLANES:]], axis=0)


def _norm_mm_kernel(x_ref, g_ref, w_ref, o_ref, *, precise):
    x = x_ref[...]
    ms = jnp.mean(x * x, axis=-1, keepdims=True)
    xn = x * lax.rsqrt(ms + EPS) * g_ref[...]
    o_ref[...] = _mm(xn, w_ref[...], precise)


def norm_matmul(x, g, w, precise=False):
    M, K = x.shape
    N = w.shape[1]
    tm = _pick_tile(M, (512, 256, 128))
    return pl.pallas_call(
        functools.partial(_norm_mm_kernel, precise=precise),
        grid=(M // tm,),
        in_specs=[pl.BlockSpec((tm, K), lambda i: (i, 0)),
                  pl.BlockSpec((1, K), lambda i: (0, 0)),
                  pl.BlockSpec((K, N), lambda i: (0, 0))],
        out_specs=pl.BlockSpec((tm, N), lambda i: (i, 0)),
        out_shape=jax.ShapeDtypeStruct((M, N), F32),
        compiler_params=_cparams(1),
        name="norm_matmul",
    )(x, g.reshape(1, K), w)


def _mm_res_kernel(a_ref, w_ref, r_ref, o_ref, *, precise):
    o_ref[...] = r_ref[...] + _mm(a_ref[...], w_ref[...], precise)


def matmul_residual(a, w, res, precise=False):
    M, K = a.shape
    N = w.shape[1]
    tm = _pick_tile(M, (1024, 768, 512, 256, 128))
    return pl.pallas_call(
        functools.partial(_mm_res_kernel, precise=precise),
        grid=(M // tm,),
        in_specs=[pl.BlockSpec((tm, K), lambda i: (i, 0)),
                  pl.BlockSpec((K, N), lambda i: (0, 0)),
                  pl.BlockSpec((tm, N), lambda i: (i, 0))],
        out_specs=pl.BlockSpec((tm, N), lambda i: (i, 0)),
        out_shape=jax.ShapeDtypeStruct((M, N), F32),
        compiler_params=_cparams(1),
        name="matmul_residual",
    )(a, w, res)


def _ffn_kernel(*refs, gated, precise):
    if gated:
        x_ref, g_ref, comb_ref, w1_ref, w3_ref, w2_ref, o_ref, xn_ref, acc_ref = refs
    else:
        x_ref, g_ref, w1_ref, w3_ref, w2_ref, o_ref, xn_ref, acc_ref = refs
    e = pl.program_id(1)
    f = pl.program_id(2)
    first = jnp.logical_and(e == 0, f == 0)
    last = jnp.logical_and(e == pl.num_programs(1) - 1, f == pl.num_programs(2) - 1)

    @pl.when(first)
    def _():
        x = x_ref[...]
        ms = jnp.mean(x * x, axis=-1, keepdims=True)
        xn_ref[...] = (x * lax.rsqrt(ms + EPS) * g_ref[...]).astype(xn_ref.dtype)
        acc_ref[...] = jnp.zeros_like(acc_ref)

    xn = xn_ref[...]
    h1 = _mm(xn, w1_ref[0], precise)
    h3 = _mm(xn, w3_ref[0], precise)
    a = h1 * _sigmoid(h1) * h3
    if gated:
        comb = comb_ref[...]
        lane = lax.broadcasted_iota(jnp.int32, comb.shape, 1)
        a = a * jnp.sum(jnp.where(lane == e, comb, 0.0), axis=-1, keepdims=True)
    acc_ref[...] += _mm(a, w2_ref[0], precise)

    @pl.when(last)
    def _():
        o_ref[...] = x_ref[...] + acc_ref[...]


def swiglu_residual(x, g, w1, w3, w2, comb=None, precise=False):
    M, D = x.shape
    E, _, F = w1.shape
    tm = _pick_tile(M, (1024, 768, 512, 256, 128))
    tf = _pick_tile(F, (256, 128))
    gated = comb is not None
    in_specs = [pl.BlockSpec((tm, D), lambda i, e, f: (i, 0)),
                pl.BlockSpec((1, D), lambda i, e, f: (0, 0))]
    args = [x, g.reshape(1, D)]
    if gated:
        in_specs.append(pl.BlockSpec((tm, LANES), lambda i, e, f: (i, 0)))
        args.append(comb)
    in_specs += [pl.BlockSpec((1, D, tf), lambda i, e, f: (e, 0, f)),
                 pl.BlockSpec((1, D, tf), lambda i, e, f: (e, 0, f)),
                 pl.BlockSpec((1, tf, D), lambda i, e, f: (e, f, 0))]
    args += [w1, w3, w2]
    return pl.pallas_call(
        functools.partial(_ffn_kernel, gated=gated, precise=precise),
        grid=(M // tm, E, F // tf),
        in_specs=in_specs,
        out_specs=pl.BlockSpec((tm, D), lambda i, e, f: (i, 0)),
        out_shape=jax.ShapeDtypeStruct((M, D), F32),
        scratch_shapes=[pltpu.VMEM((tm, D), F32 if precise else BF16), pltpu.VMEM((tm, D), F32)],
        compiler_params=_cparams(3),
        name="swiglu_gated" if gated else "swiglu",
    )(*args)


def _router_kernel(x_ref, g_ref, w_ref, o_ref, *, n_experts):
    x = x_ref[...]
    ms = jnp.mean(x * x, axis=-1, keepdims=True)
    logits = _dot3(x * lax.rsqrt(ms + EPS) * g_ref[...], w_ref[...])
    lane = lax.broadcasted_iota(jnp.int32, logits.shape, 1)
    lg = jnp.where(lane < n_experts, logits, -jnp.inf)
    v1 = jnp.max(lg, axis=-1, keepdims=True)
    lane_f = lane.astype(F32)
    i1 = jnp.min(jnp.where(lg == v1, lane_f, float(LANES)), axis=-1, keepdims=True)
    lg2 = jnp.where(lane_f == i1, -jnp.inf, lg)
    v2 = jnp.max(lg2, axis=-1, keepdims=True)
    i2 = jnp.min(jnp.where(lg2 == v2, lane_f, float(LANES)), axis=-1, keepdims=True)
    ex = jnp.exp(v2 - v1)
    g1 = 1.0 / (1.0 + ex)
    g2 = ex / (1.0 + ex)
    o_ref[...] = jnp.where(lane_f == i1, g1, 0.0) + jnp.where(lane_f == i2, g2, 0.0)


def router(x, g, w_router):
    M, D = x.shape
    E = w_router.shape[1]
    tm = _pick_tile(M, (512, 256, 128))
    return pl.pallas_call(
        functools.partial(_router_kernel, n_experts=E),
        grid=(M // tm,),
        in_specs=[pl.BlockSpec((tm, D), lambda i: (i, 0)),
                  pl.BlockSpec((1, D), lambda i: (0, 0)),
                  pl.BlockSpec((D, LANES), lambda i: (0, 0))],
        out_specs=pl.BlockSpec((tm, LANES), lambda i: (i, 0)),
        out_shape=jax.ShapeDtypeStruct((M, LANES), F32),
        compiler_params=_cparams(1),
        name="router",
    )(x, g.reshape(1, D), jnp.pad(w_router, ((0, 0), (0, LANES - E))))


def _rope_head(xh, tc, ts1, ts2):
    return xh * tc + pltpu.roll(xh, 16, 1) * ts1 + pltpu.roll(xh, LANES - 16, 1) * ts2


def _mla_prep_kernel(cq_ref, kr_ref, ckv_ref, tc_ref, ts1_ref, ts2_ref,
                     qag_ref, kvg_ref, qng_ref, kng_ref, wuq_ref, wuk_ref, wuv_ref,
                     q_ref, k_ref, v_ref, rows_ref, *, precise):
    tc, ts1, ts2 = tc_ref[...], ts1_ref[...], ts2_ref[...]
    cq = cq_ref[...]
    cqn = cq * lax.rsqrt(jnp.mean(cq * cq, axis=-1, keepdims=True) + EPS) * qag_ref[...]
    qraw = _mm(cqn, wuq_ref[...], precise)
    ckv = ckv_ref[...]
    lat = ckv * lax.rsqrt(jnp.mean(ckv * ckv, axis=-1, keepdims=True) + EPS) * kvg_ref[...]
    kraw = _mm(lat, wuk_ref[...], precise)
    v_ref[...] = _mm(lat, wuv_ref[...], precise).astype(v_ref.dtype)
    krr = _rope_head(pltpu.roll(kr_ref[...], QK_NOPE, 1), tc, ts1, ts2)
    lane = lax.broadcasted_iota(jnp.int32, krr.shape, 1)
    krr = jnp.where(jnp.logical_and(lane >= QK_NOPE, lane < QK_HEAD), krr, 0.0)
    rows_ref[:, :KV_LORA] = lat
    rows_ref[:, KV_LORA:] = pltpu.roll(krr, LANES - QK_NOPE, 1)[:, :QK_ROPE]
    qng = qng_ref[...]
    kng = kng_ref[...]
    for h in range(MLA_HEADS):
        sl = slice(h * HEAD_PAD, (h + 1) * HEAD_PAD)
        qh = _rope_head(qraw[:, sl], tc, ts1, ts2)
        qs = jnp.sum(qh * qh, axis=-1, keepdims=True) * (1.0 / QK_HEAD)
        q_ref[:, sl] = (qh * lax.rsqrt(qs + EPS) * qng).astype(q_ref.dtype)
        kh = kraw[:, sl] + krr
        ks = jnp.sum(kh * kh, axis=-1, keepdims=True) * (1.0 / QK_HEAD)
        k_ref[:, sl] = (kh * lax.rsqrt(ks + EPS) * kng).astype(k_ref.dtype)


def mla_prep(z, tabs, qag, kvg, qng, kng, wuq, wuk, wuv, precise=False):
    M = z.shape[0]
    tm = _pick_tile(M, (512, 256, 128))
    HP = MLA_HEADS * HEAD_PAD
    odt = F32 if precise else BF16
    col = lambda off, w: pl.BlockSpec((tm, w), lambda i: (i, off // w))
    row = lambda w: pl.BlockSpec((1, w), lambda i: (0, 0))
    full = lambda a: pl.BlockSpec(a.shape, lambda i: (0, 0))
    tab = pl.BlockSpec((tm, LANES), lambda i: (i, 0))
    return pl.pallas_call(
        functools.partial(_mla_prep_kernel, precise=precise),
        grid=(M // tm,),
        in_specs=[col(Z_CQ, Q_LORA), col(Z_KR, LANES), col(Z_CKV, KV_LORA), tab, tab, tab,
                  row(Q_LORA), row(KV_LORA), row(LANES), row(LANES),
                  full(wuq), full(wuk), full(wuv)],
        out_specs=[pl.BlockSpec((tm, HP), lambda i: (i, 0)),
                   pl.BlockSpec((tm, HP), lambda i: (i, 0)),
                   pl.BlockSpec((tm, MLA_HEADS * V_HEAD), lambda i: (i, 0)),
                   pl.BlockSpec((tm, LAT_W), lambda i: (i, 0))],
        out_shape=[jax.ShapeDtypeStruct((M, HP), odt),
                   jax.ShapeDtypeStruct((M, HP), odt),
                   jax.ShapeDtypeStruct((M, MLA_HEADS * V_HEAD), odt),
                   jax.ShapeDtypeStruct((M, LAT_W), F32)],
        compiler_params=_cparams(1),
        name="mla_prep",
    )(z, z, z, *tabs, qag, kvg, qng, kng, wuq, wuk, wuv)


def _attn_prompt_kernel(q_ref, k_ref, v_ref, o_ref, *, tq):
    i = pl.program_id(1)
    rid = lax.broadcasted_iota(jnp.int32, (tq, tq), 0)
    cid = lax.broadcasted_iota(jnp.int32, (tq, tq), 1)
    causal = cid <= rid
    for h in range(MLA_HEADS):
        ksl = slice(h * HEAD_PAD, (h + 1) * HEAD_PAD)
        vsl = slice(h * V_HEAD, (h + 1) * V_HEAD)
        qh = q_ref[:, ksl]

        def step(s, vblk, carry):
            m, l, acc = carry
            m_new = jnp.maximum(m, jnp.max(s, axis=-1, keepdims=True))
            p = jnp.exp(s - m_new)
            alpha = jnp.exp(m - m_new)
            l_new = alpha * l + jnp.sum(p, axis=-1, keepdims=True)
            acc_new = alpha * acc + _dot(p.astype(BF16), vblk)
            return m_new, l_new, acc_new

        def body(j, carry):
            r0 = pl.multiple_of(j * tq, tq)
            s = _dot_nt(qh, k_ref[pl.ds(r0, tq), ksl])
            return step(s, v_ref[pl.ds(r0, tq), vsl], carry)

        init = (jnp.full((tq, 1), NEG, F32), jnp.zeros((tq, 1), F32), jnp.zeros((tq, V_HEAD), F32))
        carry = lax.fori_loop(0, i, body, init)
        r0 = pl.multiple_of(i * tq, tq)
        s = jnp.where(causal, _dot_nt(qh, k_ref[pl.ds(r0, tq), ksl]), NEG)
        m, l, acc = step(s, v_ref[pl.ds(r0, tq), vsl], carry)
        o_ref[:, vsl] = (acc / l).astype(BF16)


def attn_prompt(q, k, v, B, TP):
    tq = _pick_tile(TP, (384, 256, 128))
    nq = TP // tq
    HP = MLA_HEADS * HEAD_PAD
    VW = MLA_HEADS * V_HEAD
    return pl.pallas_call(
        functools.partial(_attn_prompt_kernel, tq=tq),
        grid=(B, nq),
        in_specs=[pl.BlockSpec((tq, HP), lambda b, i: (b * nq + i, 0)),
                  pl.BlockSpec((TP, HP), lambda b, i: (b, 0)),
                  pl.BlockSpec((TP, VW), lambda b, i: (b, 0))],
        out_specs=pl.BlockSpec((tq, VW), lambda b, i: (b * nq + i, 0)),
        out_shape=jax.ShapeDtypeStruct((B * TP, VW), BF16),
        compiler_params=_cparams(2),
        name="attn_prompt",
    )(q, k, v)


def _attn_sample_kernel(pt_ref, qn_ref, qr_ref, nrow_ref, cache_ref, wukt_ref, wuv_ref,
                        o_ref, buf_ref, sem_ref, l_ref, kr_ref, nr_ref, sc_ref, latb_ref,
                        *, layer, n_pages, chunk):
    b = pl.program_id(0)
    nb = pl.num_programs(0)
    n_chunks = n_pages * PAGE // chunk
    slot = b % 2
    H = MLA_HEADS
    NK = MLA_HEADS * QK_NOPE

    def page_copy(seq, p, s):
        return pltpu.make_async_copy(cache_ref.at[layer, pt_ref[seq, p]],
                                     buf_ref.at[s, pl.ds(p * PAGE, PAGE)],
                                     sem_ref.at[s])

    def start_all(seq, s):
        for p in range(n_pages):
            page_copy(seq, p, s).start()

    @pl.when(b == 0)
    def _():
        start_all(0, 0)
        l_ref[:NK, :] = wukt_ref[...]
        kr_ref[...] = jnp.zeros_like(kr_ref)
        nr_ref[...] = jnp.zeros_like(nr_ref)

    @pl.when(b + 1 < nb)
    def _():
        start_all(b + 1, 1 - slot)

    l_ref[NK:, :] = _dot(qn_ref[0], wukt_ref[...]).astype(BF16)
    qr = qr_ref[0]
    ones = jnp.ones((2 * H, LANES), BF16)
    hid = lax.broadcasted_iota(jnp.int32, (H, 1), 0)

    def scores(rows, n_rows):
        latb = rows[:, :KV_LORA].astype(BF16)
        kr_ref[:n_rows, :QK_ROPE] = rows[:, KV_LORA:]
        kr = kr_ref[:n_rows, :]
        kt = _dot_nt(l_ref[...], latb)
        raw = kt[NK:NK + H, :] + _dot_nt(qr, kr.astype(BF16))[:H]
        krsq = _dot_nt(ones, (kr * kr).astype(BF16))[:H]
        ss = jnp.zeros((H, n_rows), F32)
        for h in range(H):
            blk = kt[h * QK_NOPE:(h + 1) * QK_NOPE, :]
            ss = ss + jnp.where(hid == h, jnp.sum(blk * blk, axis=0, keepdims=True), 0.0)
        return raw * lax.rsqrt((ss + krsq) * (1.0 / QK_HEAD) + EPS), latb

    nr_ref[0:1, :] = nrow_ref[0]
    sc_new, latb_new = scores(nr_ref[...], PAGE)
    sc_new = jnp.where(lax.broadcasted_iota(jnp.int32, sc_new.shape, 1) == 0, sc_new, NEG)

    for p in range(n_pages):
        page_copy(b, p, slot).wait()

    def chunk_body(c, _):
        r0 = pl.multiple_of(c * chunk, chunk)
        sc, latb = scores(buf_ref[slot, pl.ds(r0, chunk), :], chunk)
        sc_ref[c] = sc
        latb_ref[pl.ds(r0, chunk), :] = latb
        return 0

    lax.fori_loop(0, n_chunks, chunk_body, 0)

    sc = sc_ref[...]
    mx = jnp.maximum(jnp.max(jnp.max(sc, axis=-1, keepdims=True), axis=0),
                     jnp.max(sc_new, axis=-1, keepdims=True))
    p_new = jnp.exp(sc_new - mx)
    l = jnp.sum(p_new, axis=-1, keepdims=True)
    zpad = jnp.zeros((H, PAGE), F32)
    o_lat = _dot(jnp.concatenate([p_new, zpad], axis=0).astype(BF16), latb_new)[:H]
    zpad = jnp.zeros((H, chunk), F32)
    for c in range(n_chunks):
        pc = jnp.exp(sc[c] - mx)
        l = l + jnp.sum(pc, axis=-1, keepdims=True)
        o_lat = o_lat + _dot(jnp.concatenate([pc, zpad], axis=0).astype(BF16),
                             latb_ref[c * chunk:(c + 1) * chunk, :])[:H]
    o_lat = o_lat / l
    full = _dot(jnp.concatenate([o_lat, jnp.zeros_like(o_lat)], axis=0).astype(BF16), wuv_ref[...])[:H]
    cid = lax.broadcasted_iota(jnp.int32, full.shape, 1)
    rid = lax.broadcasted_iota(jnp.int32, full.shape, 0)
    o_ref[0] = jnp.sum(jnp.where(cid // V_HEAD == rid, full, 0.0), axis=0, keepdims=True)


def attn_sample(qn, qr, new_rows, cache, layer, page_table, wukt, wuv):
    Bs, n_pages = page_table.shape
    t_past = n_pages * PAGE
    chunk = _pick_tile(t_past, (1024, 512, 256, 128))
    H2 = 2 * MLA_HEADS
    NK = MLA_HEADS * QK_NOPE
    VW = MLA_HEADS * V_HEAD
    grid_spec = pltpu.PrefetchScalarGridSpec(
        num_scalar_prefetch=1,
        grid=(Bs,),
        in_specs=[pl.BlockSpec((1, H2, NK), lambda b, pt: (b, 0, 0)),
                  pl.BlockSpec((1, H2, LANES), lambda b, pt: (b, 0, 0)),
                  pl.BlockSpec((1, 1, LAT_W), lambda b, pt: (b, 0, 0)),
                  pl.BlockSpec(memory_space=pl.ANY),
                  pl.BlockSpec(wukt.shape, lambda b, pt: (0, 0)),
                  pl.BlockSpec(wuv.shape, lambda b, pt: (0, 0))],
        out_specs=pl.BlockSpec((1, 1, VW), lambda b, pt: (b, 0, 0)),
        scratch_shapes=[pltpu.VMEM((2, t_past, LAT_W), F32),
                        pltpu.SemaphoreType.DMA((2,)),
                        pltpu.VMEM((NK + H2, KV_LORA), BF16),
                        pltpu.VMEM((chunk, LANES), F32),
                        pltpu.VMEM((PAGE, LAT_W), F32),
                        pltpu.VMEM((t_past // chunk, MLA_HEADS, chunk), F32),
                        pltpu.VMEM((t_past, KV_LORA), BF16)])
    out = pl.pallas_call(
        functools.partial(_attn_sample_kernel, layer=layer, n_pages=n_pages, chunk=chunk),
        grid_spec=grid_spec,
        out_shape=jax.ShapeDtypeStruct((Bs, 1, VW), F32),
        compiler_params=_cparams(1),
        name="attn_sample",
    )(page_table, qn, qr, new_rows.reshape(Bs, 1, LAT_W), cache, wukt, wuv)
    return out.reshape(Bs, VW)


def _mlstm_prompt_kernel(u_ref, v_ref, og_ref, gt_ref, cw_ref, cb_ref, wq_ref, wk_ref, gb_ref, ng_ref,
                         hm_ref, c_out, n_out, m_out,
                         uprev_ref, cs_ref, ns_ref, ms_ref, *, t_valid):
    L = SEQ_BLOCK
    i = pl.program_id(1)

    @pl.when(i == 0)
    def _():
        uprev_ref[...] = jnp.zeros_like(uprev_ref)
        cs_ref[...] = jnp.zeros_like(cs_ref)
        ns_ref[...] = jnp.zeros_like(ns_ref)
        ms_ref[...] = jnp.zeros_like(ms_ref)

    u = u_ref[...]
    prev8 = uprev_ref[...]
    cw = cw_ref[...]
    uc = cb_ref[...] + cw[3:4] * u
    for k in range(1, CONV_W):
        uc = uc + cw[3 - k:4 - k] * _shift_rows(u, prev8, k)
    uprev_ref[...] = u[L - SUBLANES:, :]
    uc = uc * _sigmoid(uc)

    gb = gt_ref[...] + gb_ref[...]
    row = lax.broadcasted_iota(jnp.int32, (L, 1), 0) + i * L
    valid = row < t_valid
    li_all = jnp.where(valid, gb, NEG)
    lf_all = jnp.where(valid, _log_sigmoid(gb), 0.0)
    rid = lax.broadcasted_iota(jnp.int32, (L, L), 0)
    cid = lax.broadcasted_iota(jnp.int32, (L, L), 1)
    causal = cid <= rid
    tri = causal.astype(BF16)
    lf_hi, lf_lo = _split_bf16(lf_all)
    b_all = _dot(tri, lf_hi) + _dot(tri, lf_lo)
    li_t = li_all.T
    b_t = b_all.T

    for h in range(ML_HEADS):
        sl = slice(h * ML_HEAD, (h + 1) * ML_HEAD)
        uh = uc[:, sl].astype(BF16)
        qh = _dot(uh, wq_ref[h])
        kh = _dot(uh, wk_ref[h]) * (ML_HEAD ** -0.5)
        vh = v_ref[:, sl]
        qb, kb = qh.astype(BF16), kh.astype(BF16)
        li_col = li_all[:, h:h + 1]
        li_row = li_t[h:h + 1, :]
        b_col = b_all[:, ML_HEADS + h:ML_HEADS + h + 1]
        b_row = b_t[ML_HEADS + h:ML_HEADS + h + 1, :]
        m_prev = ms_ref[h:h + 1, 0:1]
        n_row = ns_ref[h:h + 1, :]
        c_prev = cs_ref[h]

        dm = jnp.where(causal, b_col - b_row + li_row, NEG)
        inter = b_col + m_prev
        m_t = jnp.maximum(inter, jnp.max(dm, axis=-1, keepdims=True))
        w_intra = jnp.exp(dm - m_t)
        w_inter = jnp.exp(inter - m_t)
        s = _dot_nt(qb, kb) * w_intra
        num = _dot(s.astype(BF16), vh.astype(BF16)) + w_inter * _dot_nt(qb, c_prev.astype(BF16))
        den = jnp.sum(s, axis=-1, keepdims=True) + w_inter * jnp.sum(qh * n_row, axis=-1, keepdims=True)
        hh = num / jnp.maximum(jnp.abs(den), jnp.exp(-m_t))

        b_end = b_col[L - 1:L, :]
        g_col = b_end - b_col + li_col
        g_row = b_end - b_row + li_row
        m_new = jnp.maximum(b_end + m_prev, jnp.max(g_row, axis=-1, keepdims=True))
        wg_col = jnp.exp(g_col - m_new)
        wc = jnp.exp(b_end + m_prev - m_new)
        cs_ref[h] = wc * c_prev + _dot_tn((vh * wg_col).astype(BF16), kb)
        ns_ref[h:h + 1, :] = wc * n_row + jnp.sum(kh * wg_col, axis=0, keepdims=True)
        ms_ref[h:h + 1, :] = jnp.broadcast_to(m_new, (1, LANES))

        hn = hh * lax.rsqrt(jnp.mean(hh * hh, axis=-1, keepdims=True) + EPS) * ng_ref[h:h + 1, :]
        hm_ref[:, sl] = (hn * _sigmoid(og_ref[:, sl])).astype(BF16)

    @pl.when(i == pl.num_programs(1) - 1)
    def _():
        c_out[0] = cs_ref[...]
        n_out[0] = ns_ref[:ML_HEADS, :]
        m_out[0] = ms_ref[:ML_HEADS, :]


def mlstm_prompt(z, cw, cb, wq, wk, gate_bias, ng, B, TP, t_valid):
    nc = TP // SEQ_BLOCK
    zcol = lambda off, w: pl.BlockSpec((SEQ_BLOCK, w), lambda b, i: (b * nc + i, off // w))
    full = lambda a: pl.BlockSpec(a.shape, lambda b, i: (0,) * a.ndim)
    return pl.pallas_call(
        functools.partial(_mlstm_prompt_kernel, t_valid=t_valid),
        grid=(B, nc),
        in_specs=[zcol(Z_U, ML_W), zcol(Z_V, ML_W), zcol(Z_OG, ML_W), zcol(Z_GT, LANES),
                  full(cw), full(cb), full(wq), full(wk), full(gate_bias), full(ng)],
        out_specs=[pl.BlockSpec((SEQ_BLOCK, ML_W), lambda b, i: (b * nc + i, 0)),
                   pl.BlockSpec((1, ML_HEADS, ML_HEAD, ML_HEAD), lambda b, i: (b, 0, 0, 0)),
                   pl.BlockSpec((1, ML_HEADS, ML_HEAD), lambda b, i: (b, 0, 0)),
                   pl.BlockSpec((1, ML_HEADS, LANES), lambda b, i: (b, 0, 0))],
        out_shape=[jax.ShapeDtypeStruct((B * TP, ML_W), BF16),
                   jax.ShapeDtypeStruct((B, ML_HEADS, ML_HEAD, ML_HEAD), F32),
                   jax.ShapeDtypeStruct((B, ML_HEADS, ML_HEAD), F32),
                   jax.ShapeDtypeStruct((B, ML_HEADS, LANES), F32)],
        scratch_shapes=[pltpu.VMEM((SUBLANES, ML_W), F32),
                        pltpu.VMEM((ML_HEADS, ML_HEAD, ML_HEAD), F32),
                        pltpu.VMEM((SUBLANES, ML_HEAD), F32),
                        pltpu.VMEM((SUBLANES, LANES), F32)],
        compiler_params=_cparams(2),
        name="mlstm_prompt",
    )(z, z, z, z, cw, cb, wq, wk, gate_bias, ng)


def _mlstm_sample_kernel(u_ref, v_ref, og_ref, gt_ref, buf_ref, c_ref, n_ref, m_ref,
                         cw_ref, cb_ref, wq_ref, wk_ref, gb_ref, ng_ref,
                         hm_ref, c_out, n_out, m_out, buf_out):
    u = u_ref[0]
    buf = buf_ref[0]
    cw = cw_ref[...]
    uc = cb_ref[...] + cw[3:4] * u
    for j in range(CONV_W - 1):
        uc = uc + cw[j:j + 1] * buf[j:j + 1]
    buf_out[0, 0:CONV_W - 2, :] = buf[1:CONV_W - 1]
    buf_out[0, CONV_W - 2:CONV_W - 1, :] = u
    uc = uc * _sigmoid(uc)
    gb = gt_ref[0] + gb_ref[...]
    lane = lax.broadcasted_iota(jnp.int32, (1, LANES), 1)
    m_all = m_ref[0]
    m_acc = jnp.zeros((1, LANES), F32)
    for h in range(ML_HEADS):
        sl = slice(h * ML_HEAD, (h + 1) * ML_HEAD)
        uh = jnp.broadcast_to(uc[:, sl], (ROWS16, ML_HEAD))
        qh = _dot3(uh, wq_ref[h])
        kh = _dot3(uh, wk_ref[h]) * (ML_HEAD ** -0.5)
        vh = v_ref[0][:, sl]
        li = gb[:, h:h + 1]
        lf = _log_sigmoid(gb[:, ML_HEADS + h:ML_HEADS + h + 1])
        m_prev = m_all[:, h:h + 1]
        c_prev = c_ref[0, h]
        n_row = n_ref[0, h:h + 1, :]
        m_t = jnp.maximum(lf + m_prev, li)
        w_in = jnp.exp(li - m_t)
        w_st = jnp.exp(lf + m_prev - m_t)
        s = jnp.sum(qh[0:1] * kh[0:1], axis=-1, keepdims=True) * w_in
        cq = _dot3(qh, c_prev, _dot_nt)[0:1]
        num = s * vh + w_st * cq
        den = s + w_st * jnp.sum(qh[0:1] * n_row, axis=-1, keepdims=True)
        hh = num / jnp.maximum(jnp.abs(den), jnp.exp(-m_t))
        v16 = jnp.where(lax.broadcasted_iota(jnp.int32, (ROWS16, ML_HEAD), 0) == 0,
                        jnp.broadcast_to(vh * w_in, (ROWS16, ML_HEAD)), 0.0)
        c_out[0, h] = w_st * c_prev + _dot3(v16, kh, _dot_tn)
        n_out[0, h:h + 1, :] = w_st * n_row + w_in * kh[0:1]
        m_acc = m_acc + jnp.where(lane == h, m_t, 0.0)
        hn = hh * lax.rsqrt(jnp.mean(hh * hh, axis=-1, keepdims=True) + EPS) * ng_ref[h:h + 1, :]
        hm_ref[0, :, sl] = hn * _sigmoid(og_ref[0][:, sl])
    m_out[0] = m_acc


def mlstm_sample(zs, buf, C, n, m, cw, cb, wq, wk, gate_bias, ng):
    Bs = zs.shape[0]
    mp = jnp.pad(m, ((0, 0), (0, LANES - ML_HEADS))).reshape(Bs, 1, LANES)
    zcol = lambda off, w: pl.BlockSpec((1, 1, w), lambda b: (b, 0, off // w))
    full = lambda a: pl.BlockSpec(a.shape, lambda b: (0,) * a.ndim)
    seq = lambda a: pl.BlockSpec((1,) + a.shape[1:], lambda b: (b,) + (0,) * (a.ndim - 1))
    outs = pl.pallas_call(
        _mlstm_sample_kernel,
        grid=(Bs,),
        in_specs=[zcol(Z_U, ML_W), zcol(Z_V, ML_W), zcol(Z_OG, ML_W), zcol(Z_GT, LANES),
                  seq(buf), seq(C), seq(n), seq(mp),
                  full(cw), full(cb), full(wq), full(wk), full(gate_bias), full(ng)],
        out_specs=[pl.BlockSpec((1, 1, ML_W), lambda b: (b, 0, 0)),
                   seq(C), seq(n), seq(mp), seq(buf)],
        out_shape=[jax.ShapeDtypeStruct((Bs, 1, ML_W), F32),
                   jax.ShapeDtypeStruct(C.shape, F32),
                   jax.ShapeDtypeStruct(n.shape, F32),
                   jax.ShapeDtypeStruct(mp.shape, F32),
                   jax.ShapeDtypeStruct(buf.shape, F32)],
        compiler_params=_cparams(1),
        name="mlstm_sample",
    )(zs, zs, zs, zs, buf, C, n, mp, cw, cb, wq, wk, gate_bias, ng)
    hm, C_new, n_new, m_new, buf_new = outs
    return hm.reshape(Bs, ML_W), C_new, n_new, m_new.reshape(Bs, LANES)[:, :ML_HEADS], buf_new


def _lru_gates(xc, wa_ref, wx_ref, ba, bx, c, precise=False):
    rs, iss = [], []
    for g in range(LRU_BLOCKS):
        xg = xc[:, g * LANES:(g + 1) * LANES]
        rs.append(_mm(xg, wa_ref[g], precise))
        iss.append(_mm(xg, wx_ref[g], precise))
    r = _sigmoid(jnp.concatenate(rs, axis=-1) + ba)
    ig = _sigmoid(jnp.concatenate(iss, axis=-1) + bx)
    a = jnp.exp(c * r)
    return a, jnp.sqrt(1.0 - a * a) * ig * xc


def _rglru_prompt_kernel(gate_ref, xb_ref, cw_ref, cb_ref, wa_ref, wx_ref, ba_ref, bx_ref, c_ref,
                         y_ref, h_out, xprev_ref, hs_ref, *, t_valid):
    T = SEQ_BLOCK
    i = pl.program_id(1)

    @pl.when(i == 0)
    def _():
        xprev_ref[...] = jnp.zeros_like(xprev_ref)
        hs_ref[...] = jnp.zeros_like(hs_ref)

    xb = xb_ref[...]
    prev8 = xprev_ref[...]
    cw = cw_ref[...]
    xc = cb_ref[...] + cw[3:4] * xb
    for k in range(1, CONV_W):
        xc = xc + cw[3 - k:4 - k] * _shift_rows(xb, prev8, k)
    xprev_ref[...] = xb[T - SUBLANES:, :]

    a, bx = _lru_gates(xc, wa_ref, wx_ref, ba_ref[...], bx_ref[...], c_ref[...])
    row = lax.broadcasted_iota(jnp.int32, (T, 1), 0)
    valid = row + i * T < t_valid
    a = jnp.where(valid, a, 1.0)
    bx = jnp.where(valid, bx, 0.0)
    k = 1
    while k < T:
        keep = row >= k
        a_sh = jnp.where(keep, pltpu.roll(a, k, 0), 1.0)
        b_sh = jnp.where(keep, pltpu.roll(bx, k, 0), 0.0)
        bx = a * b_sh + bx
        a = a * a_sh
        k *= 2
    hs = a * hs_ref[0:1, :] + bx
    hs_ref[...] = jnp.broadcast_to(hs[T - 1:T, :], hs_ref.shape)
    y_ref[...] = (hs * jax.nn.gelu(gate_ref[...])).astype(BF16)

    @pl.when(i == pl.num_programs(1) - 1)
    def _():
        h_out[0] = hs[T - 1:T, :]


def rglru_prompt(zz, cw, cb, wa, wx, ba, bx, c, B, TP, t_valid):
    W = zz.shape[1] // 2
    nb = TP // SEQ_BLOCK
    full = lambda a: pl.BlockSpec(a.shape, lambda b, i: (0,) * a.ndim)
    return pl.pallas_call(
        functools.partial(_rglru_prompt_kernel, t_valid=t_valid),
        grid=(B, nb),
        in_specs=[pl.BlockSpec((SEQ_BLOCK, W), lambda b, i: (b * nb + i, 0)),
                  pl.BlockSpec((SEQ_BLOCK, W), lambda b, i: (b * nb + i, 1)),
                  full(cw), full(cb), full(wa), full(wx), full(ba), full(bx), full(c)],
        out_specs=[pl.BlockSpec((SEQ_BLOCK, W), lambda b, i: (b * nb + i, 0)),
                   pl.BlockSpec((1, 1, W), lambda b, i: (b, 0, 0))],
        out_shape=[jax.ShapeDtypeStruct((B * TP, W), BF16),
                   jax.ShapeDtypeStruct((B, 1, W), F32)],
        scratch_shapes=[pltpu.VMEM((SUBLANES, W), F32), pltpu.VMEM((SUBLANES, W), F32)],
        compiler_params=_cparams(2),
        name="rglru_prompt",
    )(zz, zz, cw, cb, wa, wx, ba, bx, c)


def _rglru_sample_kernel(gate_ref, xb_ref, b0_ref, b1_ref, b2_ref, h0_ref,
                         cw_ref, cb_ref, wa_ref, wx_ref, ba_ref, bx_ref, c_ref,
                         y_ref, h_out):
    xb = xb_ref[...]
    cw = cw_ref[...]
    xc = cb_ref[...] + cw[0:1] * b0_ref[...] + cw[1:2] * b1_ref[...] + cw[2:3] * b2_ref[...] + cw[3:4] * xb
    a, bx = _lru_gates(xc, wa_ref, wx_ref, ba_ref[...], bx_ref[...], c_ref[...], precise=True)
    hs = a * h0_ref[...] + bx
    h_out[...] = hs
    y_ref[...] = hs * jax.nn.gelu(gate_ref[...])


def rglru_sample(gate, xb, buf, h0, cw, cb, wa, wx, ba, bx, c):
    Bs, W = xb.shape
    args = (gate, xb, buf[:, 0], buf[:, 1], buf[:, 2], h0, cw, cb, wa, wx, ba, bx, c)
    full = lambda a: pl.BlockSpec(a.shape, lambda i: (0,) * a.ndim)
    return pl.pallas_call(
        _rglru_sample_kernel,
        grid=(1,),
        in_specs=[full(a) for a in args],
        out_specs=[pl.BlockSpec((Bs, W), lambda i: (0, 0)), pl.BlockSpec((Bs, W), lambda i: (0, 0))],
        out_shape=[jax.ShapeDtypeStruct((Bs, W), F32), jax.ShapeDtypeStruct((Bs, W), F32)],
        compiler_params=_cparams(1),
        name="rglru_sample",
    )(*args)


def _pad_cols(w, n):
    return jnp.pad(w, ((0, 0), (0, n - w.shape[1])))


def _even_in_proj(w):
    o = 0
    pieces = {}
    for name, width in (("cq", Q_LORA), ("ckv", KV_LORA), ("kr", QK_ROPE), ("u", ML_W), ("v", ML_W),
                        ("og", ML_W), ("ig", ML_HEADS), ("fg", ML_HEADS)):
        pieces[name] = w[:, o:o + width]
        o += width
    gates = _pad_cols(jnp.concatenate([pieces["ig"], pieces["fg"]], axis=1), Z_W - Z_GT)
    return jnp.concatenate([pieces["cq"], _pad_cols(pieces["kr"], LANES), pieces["u"], pieces["v"],
                            pieces["og"], pieces["ckv"], gates], axis=1)


def _head_pad(w):
    K = w.shape[0]
    d = w.shape[1] // MLA_HEADS
    w3 = w.reshape(K, MLA_HEADS, d)
    return jnp.pad(w3, ((0, 0), (0, 0), (0, HEAD_PAD - d))).reshape(K, MLA_HEADS * HEAD_PAD)


def _rope_tables(pos):
    half = QK_ROPE // 2
    inv_freq = ROPE_THETA ** (-jnp.arange(half, dtype=F32) / half)
    ang = pos.astype(F32)[:, None] * inv_freq[None, :]
    cos, sin = jnp.cos(ang), jnp.sin(ang)
    n = pos.shape[0]
    one = jnp.ones((n, QK_NOPE), F32)
    zh = jnp.zeros((n, half), F32)
    zn = jnp.zeros((n, QK_NOPE), F32)
    zt = jnp.zeros((n, HEAD_PAD - QK_HEAD), F32)
    tc = jnp.concatenate([one, cos, cos, zt], axis=1)
    ts1 = jnp.concatenate([zn, zh, sin, zt], axis=1)
    ts2 = jnp.concatenate([zn, -sin, zh, zt], axis=1)
    return tc, ts1, ts2


def kernel(x_prompt, x_sample, cache_mla, state_mlstm_C, state_mlstm_n, state_mlstm_m, state_mlstm_conv, state_rglru_h, state_rglru_conv, page_table, meta_tokens, norm_mix_even, w_in_even, mla_q_a_g, mla_w_uq, mla_kv_a_g, mla_w_uk, mla_w_uv, mla_q_norm_g, mla_k_norm_g, ml_conv_w, ml_conv_b, ml_w_q, ml_w_k, ml_b_i, ml_b_f, ml_norm_g, w_out_even, norm_ffn_even, ffn_w1, ffn_w3, ffn_w2, norm_mix_odd, w_in_odd, lru_conv_w, lru_conv_b, lru_w_a, lru_b_a, lru_w_x, lru_b_x, lru_lam, w_out_odd, norm_ffn_odd, moe_router, moe_w1, moe_w3, moe_w2):
    B, S, D = x_prompt.shape
    Bs = x_sample.shape[0]
    assert x_sample.shape[1] == 1
    Tp = N_META + S
    TP = _round_up(Tp, SEQ_BLOCK)
    n_prompt = B * TP
    past_len = page_table.shape[1] * PAGE
    depth = norm_mix_even.shape[0] + norm_mix_odd.shape[0]

    meta = jnp.broadcast_to(meta_tokens[None], (B, N_META, D))
    xp = jnp.pad(jnp.concatenate([meta, x_prompt], axis=1), ((0, 0), (0, TP - Tp), (0, 0)))
    x = xp.reshape(n_prompt, D)
    xs = x_sample.reshape(Bs, D)
    tabs = _rope_tables(jnp.tile(jnp.arange(TP, dtype=jnp.int32), B))
    tabs_s = _rope_tables(jnp.full((Bs,), past_len, jnp.int32))
    bf = lambda a: a.astype(BF16)

    def prompt_view(a):
        return a.reshape((B, TP) + a.shape[1:])

    rows_p, rows_s, Cp, Cs, np_, ns, mp, ms, cbp, cbs = [], [], [], [], [], [], [], [], [], []
    hp_l, hs_l, lbp, lbs = [], [], [], []
    for layer in range(depth):
        if layer % 2 == 0:
            e = layer // 2
            w_in = _even_in_proj(w_in_even[e])
            z = norm_matmul(x, norm_mix_even[e], bf(w_in))
            zs = norm_matmul(xs, norm_mix_even[e], w_in, precise=True)
            qng = jnp.pad(mla_q_norm_g[e] * (QK_HEAD ** -0.5), (0, HEAD_PAD - QK_HEAD)).reshape(1, HEAD_PAD)
            kng = jnp.pad(mla_k_norm_g[e], (0, HEAD_PAD - QK_HEAD)).reshape(1, HEAD_PAD)
            wuk2 = mla_w_uk[e].reshape(KV_LORA, MLA_HEADS * QK_NOPE)
            wuv2 = mla_w_uv[e].reshape(KV_LORA, MLA_HEADS * V_HEAD)
            prep_g = (mla_q_a_g[e].reshape(1, Q_LORA), mla_kv_a_g[e].reshape(1, KV_LORA), qng, kng)
            prep_w = (_head_pad(mla_w_uq[e]), _head_pad(wuk2), wuv2)
            q, k, v, rows = mla_prep(z, tabs, *prep_g, *(bf(w) for w in prep_w))
            q_s, _, _, rows_sample = mla_prep(zs, tabs_s, *prep_g, *prep_w, precise=True)
            att_p = attn_prompt(q, k, v, B, TP)
            qs = q_s.reshape(Bs, MLA_HEADS, HEAD_PAD) * kng.reshape(1, 1, HEAD_PAD)
            eye = jnp.eye(MLA_HEADS, dtype=F32)
            qn_bd = (qs[:, :, None, :QK_NOPE] * eye[None, :, :, None]).reshape(Bs, MLA_HEADS, MLA_HEADS * QK_NOPE)
            qr_p = jnp.pad(qs[:, :, QK_NOPE:QK_HEAD], ((0, 0), (0, 0), (0, LANES - QK_ROPE)))
            pad_heads = lambda a: bf(jnp.pad(a, ((0, 0), (0, MLA_HEADS), (0, 0))))
            att_s = attn_sample(pad_heads(qn_bd), pad_heads(qr_p), rows_sample, cache_mla, e, page_table,
                                bf(wuk2.T), bf(wuv2))
            gate_bias = jnp.pad(jnp.concatenate([ml_b_i[e], ml_b_f[e]]), (0, LANES - 2 * ML_HEADS)).reshape(1, LANES)
            ml_conv = (ml_conv_w[e], ml_conv_b[e].reshape(1, ML_W))
            hm_p, c_p, n_p, m_p = mlstm_prompt(z, *ml_conv, bf(ml_w_q[e]), bf(ml_w_k[e]), gate_bias, ml_norm_g[e],
                                               B, TP, Tp)
            hm_s, c_s, n_s, m_s, cb_s = mlstm_sample(zs.reshape(Bs, 1, Z_W), state_mlstm_conv[e], state_mlstm_C[e],
                                                     state_mlstm_n[e], state_mlstm_m[e], *ml_conv,
                                                     ml_w_q[e], ml_w_k[e], gate_bias, ml_norm_g[e])
            x = matmul_residual(jnp.concatenate([att_p, hm_p], axis=1), bf(w_out_even[e]), x)
            xs = matmul_residual(jnp.concatenate([att_s, hm_s], axis=1), w_out_even[e], xs, precise=True)
            x = swiglu_residual(x, norm_ffn_even[e], bf(ffn_w1[e][None]), bf(ffn_w3[e][None]), bf(ffn_w2[e][None]))
            xs = swiglu_residual(xs, norm_ffn_even[e], ffn_w1[e][None], ffn_w3[e][None], ffn_w2[e][None],
                                 precise=True)
            rows_p.append(prompt_view(rows)[:, :Tp])
            rows_s.append(rows_sample.reshape(Bs, 1, LAT_W))
            Cp.append(c_p); np_.append(n_p); mp.append(m_p[:, :, 0])
            Cs.append(c_s); ns.append(n_s); ms.append(m_s)
            cbp.append(prompt_view(z)[:, Tp - (CONV_W - 1):Tp, Z_U:Z_U + ML_W])
            cbs.append(cb_s)
        else:
            o = layer // 2
            W = lru_conv_w.shape[2]
            zz = norm_matmul(x, norm_mix_odd[o], bf(w_in_odd[o]))
            zs = norm_matmul(xs, norm_mix_odd[o], w_in_odd[o], precise=True)
            c = (-LRU_C * jax.nn.softplus(-lru_lam[o])).reshape(1, W)
            lru_conv = (lru_conv_w[o], lru_conv_b[o].reshape(1, W))
            lru_bias = (lru_b_a[o].reshape(1, W), lru_b_x[o].reshape(1, W), c)
            y_p, h_p = rglru_prompt(zz, *lru_conv, bf(lru_w_a[o]), bf(lru_w_x[o]), *lru_bias, B, TP, Tp)
            y_s, h_s = rglru_sample(zs[:, :W], zs[:, W:], state_rglru_conv[o], state_rglru_h[o], *lru_conv,
                                    lru_w_a[o], lru_w_x[o], *lru_bias)
            x = matmul_residual(y_p, bf(w_out_odd[o]), x)
            xs = matmul_residual(y_s, w_out_odd[o], xs, precise=True)
            x = swiglu_residual(x, norm_ffn_odd[o], bf(moe_w1[o]), bf(moe_w3[o]), bf(moe_w2[o]),
                                router(x, norm_ffn_odd[o], moe_router[o]))
            xs = swiglu_residual(xs, norm_ffn_odd[o], moe_w1[o], moe_w3[o], moe_w2[o],
                                 router(xs, norm_ffn_odd[o], moe_router[o]), precise=True)
            hp_l.append(h_p[:, 0]); hs_l.append(h_s)
            lbp.append(prompt_view(zz)[:, Tp - (CONV_W - 1):Tp, W:])
            lbs.append(jnp.concatenate([state_rglru_conv[o][:, 1:], zs[:, None, W:]], axis=1))
    y_prompt = prompt_view(x)[:, N_META:Tp]
    y_sample = xs.reshape(Bs, 1, D)
    return (y_prompt, y_sample,
            jnp.stack(rows_p), jnp.stack(rows_s),
            jnp.stack(Cp), jnp.stack(Cs), jnp.stack(np_), jnp.stack(ns), jnp.stack(mp), jnp.stack(ms),
            jnp.stack(cbp), jnp.stack(cbs),
            jnp.stack(hp_l), jnp.stack(hs_l), jnp.stack(lbp), jnp.stack(lbs))
```

```python
import functools

import jax
import jax.numpy as jnp
from jax import lax
from jax.experimental import pallas as pl
from jax.experimental.pallas import tpu as pltpu

F32 = jnp.float32
BF16 = jnp.bfloat16

N_META = 16
EPS = 1e-6
MLA_HEADS = 8
Q_LORA = 384
KV_LORA = 256
QK_NOPE = 64
QK_ROPE = 32
V_HEAD = 64
QK_HEAD = QK_NOPE + QK_ROPE
LAT_W = KV_LORA + QK_ROPE
ROPE_THETA = 10000.0
ML_HEADS = 4
ML_HEAD = 128
ML_W = ML_HEADS * ML_HEAD
CONV_W = 4
LRU_BLOCKS = 8
LRU_C = 8.0
TOP_K = 2
PAGE = 128

LANES = 128
SUBLANES = 8
ROWS16 = 16
HEAD_PAD = 128
SEQ_BLOCK = 128
NEG = -1e30
VMEM_LIMIT = 56 * 1024 * 1024

R_IDX, R_GATE = 8, 10
ROW_TILE = SUBLANES * LANES

Z_CQ, Z_KR, Z_U, Z_V, Z_OG, Z_CKV, Z_GT, Z_W = 0, 384, 512, 1024, 1536, 2048, 2304, 2560


def _cparams(n_axes):
    return pltpu.CompilerParams(dimension_semantics=("arbitrary",) * n_axes,
                                vmem_limit_bytes=VMEM_LIMIT)


def _round_up(a, b):
    return (a + b - 1) // b * b


def _pick_tile(n, prefs):
    for t in prefs:
        if n % t == 0:
            return t
    if n < min(prefs):
        return n
    raise ValueError(f"no tile in {prefs} divides {n}")


def _sigmoid(x):
    return 1.0 / (1.0 + jnp.exp(-x))


def _log_sigmoid(x):
    return jnp.minimum(x, 0.0) - jnp.log(1.0 + jnp.exp(-jnp.abs(x)))


def _dot(a, b):
    return jnp.dot(a, b, preferred_element_type=F32)


def _dot_nt(a, b):
    return lax.dot_general(a, b, (((1,), (1,)), ((), ())), preferred_element_type=F32)


def _dot_tn(a, b):
    return lax.dot_general(a, b, (((0,), (0,)), ((), ())), preferred_element_type=F32)


def _split_bf16(x):
    hi = x.astype(BF16)
    lo = (x - hi.astype(F32)).astype(BF16)
    return hi, lo


def _dot3(a, b, dot=_dot):
    ah, al = _split_bf16(a)
    bh, bl = _split_bf16(b)
    return dot(ah, bh) + dot(al, bh) + dot(ah, bl)


def _mm(a, w, precise):
    if precise:
        return _dot3(a.astype(F32), w)
    return _dot(a.astype(BF16), w)


def _shift_rows(x, prev8, k):
    rolled = pltpu.roll(x, k, 0)
    head = pltpu.roll(prev8, k, 0)
    rid = lax.broadcasted_iota(jnp.int32, (SUBLANES, x.shape[1]), 0)
    top = jnp.where(rid < k, head, rolled[:SUBLANES])
    if x.shape[0] == SUBLANES:
        return top
    return jnp.concatenate([top, rolled[SUBLANES:]], axis=0)


def _norm_mm_kernel(x_ref, g_ref, w_ref, o_ref, *, precise):
    x = x_ref[...]
    ms = jnp.mean(x * x, axis=-1, keepdims=True)
    xn = x * lax.rsqrt(ms + EPS) * g_ref[...]
    o_ref[...] = _mm(xn, w_ref[...], precise)


def norm_matmul(x, g, w, precise=False):
    M, K = x.shape
    N = w.shape[1]
    tm = _pick_tile(M, (512, 256, 128))
    return pl.pallas_call(
        functools.partial(_norm_mm_kernel, precise=precise),
        grid=(M // tm,),
        in_specs=[pl.BlockSpec((tm, K), lambda i: (i, 0)),
                  pl.BlockSpec((1, K), lambda i: (0, 0)),
                  pl.BlockSpec((K, N), lambda i: (0, 0))],
        out_specs=pl.BlockSpec((tm, N), lambda i: (i, 0)),
        out_shape=jax.ShapeDtypeStruct((M, N), F32),
        compiler_params=_cparams(1),
        name="norm_matmul",
    )(x, g.reshape(1, K), w)


def _mm_res_kernel(a_ref, w_ref, r_ref, o_ref, *, precise):
    o_ref[...] = r_ref[...] + _mm(a_ref[...], w_ref[...], precise)


def matmul_residual(a, w, res, precise=False):
    M, K = a.shape
    N = w.shape[1]
    tm = _pick_tile(M, (1024, 768, 512, 256, 128))
    return pl.pallas_call(
        functools.partial(_mm_res_kernel, precise=precise),
        grid=(M // tm,),
        in_specs=[pl.BlockSpec((tm, K), lambda i: (i, 0)),
                  pl.BlockSpec((K, N), lambda i: (0, 0)),
                  pl.BlockSpec((tm, N), lambda i: (i, 0))],
        out_specs=pl.BlockSpec((tm, N), lambda i: (i, 0)),
        out_shape=jax.ShapeDtypeStruct((M, N), F32),
        compiler_params=_cparams(1),
        name="matmul_residual",
    )(a, w, res)


def _ffn_kernel(*refs, gated, precise):
    if gated:
        x_ref, g_ref, comb_ref, w1_ref, w3_ref, w2_ref, o_ref, xn_ref, acc_ref = refs
    else:
        x_ref, g_ref, w1_ref, w3_ref, w2_ref, o_ref, xn_ref, acc_ref = refs
    e = pl.program_id(1)
    f = pl.program_id(2)
    first = jnp.logical_and(e == 0, f == 0)
    last = jnp.logical_and(e == pl.num_programs(1) - 1, f == pl.num_programs(2) - 1)

    @pl.when(first)
    def _():
        x = x_ref[...]
        ms = jnp.mean(x * x, axis=-1, keepdims=True)
        xn_ref[...] = (x * lax.rsqrt(ms + EPS) * g_ref[...]).astype(xn_ref.dtype)
        acc_ref[...] = jnp.zeros_like(acc_ref)

    xn = xn_ref[...]
    h1 = _mm(xn, w1_ref[0], precise)
    h3 = _mm(xn, w3_ref[0], precise)
    a = h1 * _sigmoid(h1) * h3
    if gated:
        comb = comb_ref[...]
        lane = lax.broadcasted_iota(jnp.int32, comb.shape, 1)
        a = a * jnp.sum(jnp.where(lane == e, comb, 0.0), axis=-1, keepdims=True)
    acc_ref[...] += _mm(a, w2_ref[0], precise)

    @pl.when(last)
    def _():
        o_ref[...] = x_ref[...] + acc_ref[...]


def swiglu_residual(x, g, w1, w3, w2, comb=None, precise=False):
    M, D = x.shape
    E, _, F = w1.shape
    tm = _pick_tile(M, (1024, 768, 512, 256, 128))
    tf = _pick_tile(F, (256, 128))
    gated = comb is not None
    in_specs = [pl.BlockSpec((tm, D), lambda i, e, f: (i, 0)),
                pl.BlockSpec((1, D), lambda i, e, f: (0, 0))]
    args = [x, g.reshape(1, D)]
    if gated:
        in_specs.append(pl.BlockSpec((tm, LANES), lambda i, e, f: (i, 0)))
        args.append(comb)
    in_specs += [pl.BlockSpec((1, D, tf), lambda i, e, f: (e, 0, f)),
                 pl.BlockSpec((1, D, tf), lambda i, e, f: (e, 0, f)),
                 pl.BlockSpec((1, tf, D), lambda i, e, f: (e, f, 0))]
    args += [w1, w3, w2]
    return pl.pallas_call(
        functools.partial(_ffn_kernel, gated=gated, precise=precise),
        grid=(M // tm, E, F // tf),
        in_specs=in_specs,
        out_specs=pl.BlockSpec((tm, D), lambda i, e, f: (i, 0)),
        out_shape=jax.ShapeDtypeStruct((M, D), F32),
        scratch_shapes=[pltpu.VMEM((tm, D), F32 if precise else BF16), pltpu.VMEM((tm, D), F32)],
        compiler_params=_cparams(3),
        name="swiglu_gated" if gated else "swiglu",
    )(*args)


def _router_kernel(x_ref, g_ref, w_ref, o_ref, *, n_experts):
    x = x_ref[...]
    ms = jnp.mean(x * x, axis=-1, keepdims=True)
    logits = _dot3(x * lax.rsqrt(ms + EPS) * g_ref[...], w_ref[...])
    lane = lax.broadcasted_iota(jnp.int32, logits.shape, 1)
    lg = jnp.where(lane < n_experts, logits, -jnp.inf)
    v1 = jnp.max(lg, axis=-1, keepdims=True)
    lane_f = lane.astype(F32)
    i1 = jnp.min(jnp.where(lg == v1, lane_f, float(LANES)), axis=-1, keepdims=True)
    lg2 = jnp.where(lane_f == i1, -jnp.inf, lg)
    v2 = jnp.max(lg2, axis=-1, keepdims=True)
    i2 = jnp.min(jnp.where(lg2 == v2, lane_f, float(LANES)), axis=-1, keepdims=True)
    ex = jnp.exp(v2 - v1)
    g1 = 1.0 / (1.0 + ex)
    g2 = ex / (1.0 + ex)
    comb = jnp.where(lane_f == i1, g1, 0.0) + jnp.where(lane_f == i2, g2, 0.0)
    for col, val in ((R_IDX, i1), (R_IDX + 1, i2), (R_GATE, g1), (R_GATE + 1, g2)):
        comb = jnp.where(lane == col, val, comb)
    o_ref[...] = comb


def router(x, g, w_router):
    M, D = x.shape
    E = w_router.shape[1]
    assert E <= R_IDX
    tm = _pick_tile(M, (512, 256, 128))
    return pl.pallas_call(
        functools.partial(_router_kernel, n_experts=E),
        grid=(M // tm,),
        in_specs=[pl.BlockSpec((tm, D), lambda i: (i, 0)),
                  pl.BlockSpec((1, D), lambda i: (0, 0)),
                  pl.BlockSpec((D, LANES), lambda i: (0, 0))],
        out_specs=pl.BlockSpec((tm, LANES), lambda i: (i, 0)),
        out_shape=jax.ShapeDtypeStruct((M, LANES), F32),
        compiler_params=_cparams(1),
        name="router",
    )(x, g.reshape(1, D), jnp.pad(w_router, ((0, 0), (0, LANES - E))))


def _moe_group_kernel(te_ref, ns_ref, src_ref, x2d_ref, g_ref, gate_ref, w1_ref, w3_ref, w2_ref,
                      o_ref, buf_ref, sem_ref, xn_ref, acc_ref, *, tg, sub):
    t = pl.program_id(0)
    f = pl.program_id(1)
    nt = pl.num_programs(0)
    slot = t % 2

    def issue(tile, s):
        def body(r, _):
            tok = src_ref[tile * tg + r]
            pltpu.make_async_copy(x2d_ref.at[pl.ds(pl.multiple_of(tok * SUBLANES, SUBLANES), SUBLANES), :],
                                  buf_ref.at[s, pl.ds(pl.multiple_of(r * SUBLANES, SUBLANES), SUBLANES), :],
                                  sem_ref.at[s]).start()
            return 0
        lax.fori_loop(0, tg, body, 0)

    @pl.when(f == 0)
    def _():
        @pl.when(t == 0)
        def _():
            issue(0, 0)

        pltpu.make_async_copy(x2d_ref.at[pl.ds(0, tg * SUBLANES), :], buf_ref.at[slot], sem_ref.at[slot]).wait()

        @pl.when(t + 1 < nt)
        def _():
            issue(t + 1, 1 - slot)

        x = jnp.concatenate([buf_ref[slot, pl.ds(k, tg, stride=SUBLANES), :] for k in range(SUBLANES)], axis=1)
        ms = jnp.mean(x * x, axis=-1, keepdims=True)
        xn_ref[...] = (x * lax.rsqrt(ms + EPS) * g_ref[...]).astype(BF16)
        acc_ref[...] = jnp.zeros_like(acc_ref)

    w1 = w1_ref[0].astype(BF16)
    w3 = w3_ref[0].astype(BF16)
    w2 = w2_ref[0].astype(BF16)
    n_sub = ns_ref[t]
    for s in range(tg // sub):
        @pl.when(s < n_sub)
        def _():
            rows = slice(s * sub, (s + 1) * sub)
            xn = xn_ref[rows, :]
            h1 = _dot(xn, w1)
            h3 = _dot(xn, w3)
            acc_ref[rows, :] += _dot((h1 * _sigmoid(h1) * h3).astype(BF16), w2)

    @pl.when(f == pl.num_programs(1) - 1)
    def _():
        o_ref[...] = acc_ref[...] * gate_ref[...]


def _moe_combine_kernel(p0_ref, p1_ref, x_ref, y2d_ref, o_ref, bufa_ref, bufb_ref, sem_ref, *, tc):
    i = pl.program_id(0)

    def body(r, _):
        dst = pl.ds(pl.multiple_of(r * SUBLANES, SUBLANES), SUBLANES)
        for p_ref, buf, s in ((p0_ref, bufa_ref, 0), (p1_ref, bufb_ref, 1)):
            pos = p_ref[i * tc + r]
            pltpu.make_async_copy(y2d_ref.at[pl.ds(pl.multiple_of(pos * SUBLANES, SUBLANES), SUBLANES), :],
                                  buf.at[dst, :], sem_ref.at[s]).start()
        return 0

    lax.fori_loop(0, tc, body, 0)
    for buf, s in ((bufa_ref, 0), (bufb_ref, 1)):
        pltpu.make_async_copy(y2d_ref.at[pl.ds(0, tc * SUBLANES), :], buf, sem_ref.at[s]).wait()
    o_ref[...] = x_ref[...] + bufa_ref[...] + bufb_ref[...]


def moe_sparse_residual(x, g, comb, w1, w3, w2):
    M, D = x.shape
    assert D == ROW_TILE
    E, _, F = w1.shape
    S = TOP_K * M
    tg = 1024 if S >= 8192 else 256
    sub = tg // 4
    tf = _pick_tile(F, (512, 256, 128))
    n_tiles = S // tg + E
    P = n_tiles * tg

    e_flat = comb[:, R_IDX:R_IDX + TOP_K].astype(jnp.int32).reshape(S)
    gates = comb[:, R_GATE:R_GATE + TOP_K].reshape(S)
    onehot = (e_flat[:, None] == jnp.arange(E, dtype=jnp.int32)[None, :]).astype(jnp.int32)
    csum = jnp.cumsum(onehot, axis=0)
    counts = csum[-1]
    rank = jnp.take_along_axis(csum, e_flat[:, None], axis=1)[:, 0] - 1
    padded = (counts + tg - 1) // tg * tg
    ends = jnp.cumsum(padded)
    starts = ends - padded
    pos = starts[e_flat] + rank
    src = jnp.zeros((P,), jnp.int32).at[pos].set(jnp.arange(S, dtype=jnp.int32) // TOP_K)
    gate_sorted = jnp.zeros((P,), F32).at[pos].set(gates).reshape(P, 1)
    tile_start = jnp.arange(n_tiles, dtype=jnp.int32) * tg
    tile_e = jnp.minimum(jnp.sum((tile_start[:, None] >= ends[None, :]).astype(jnp.int32), axis=1), E - 1)
    valid = jnp.clip(starts[tile_e] + counts[tile_e] - tile_start, 0, tg)
    n_sub = (valid + sub - 1) // sub

    x2d = x.reshape(M * SUBLANES, LANES)
    grid_spec = pltpu.PrefetchScalarGridSpec(
        num_scalar_prefetch=3,
        grid=(n_tiles, F // tf),
        in_specs=[pl.BlockSpec(memory_space=pl.ANY),
                  pl.BlockSpec((1, D), lambda t, f, te, ns, sr: (0, 0)),
                  pl.BlockSpec((tg, 1), lambda t, f, te, ns, sr: (t, 0)),
                  pl.BlockSpec((1, D, tf), lambda t, f, te, ns, sr: (te[t], 0, f)),
                  pl.BlockSpec((1, D, tf), lambda t, f, te, ns, sr: (te[t], 0, f)),
                  pl.BlockSpec((1, tf, D), lambda t, f, te, ns, sr: (te[t], f, 0))],
        out_specs=pl.BlockSpec((tg, D), lambda t, f, te, ns, sr: (t, 0)),
        scratch_shapes=[pltpu.VMEM((2, tg * SUBLANES, LANES), F32),
                        pltpu.SemaphoreType.DMA((2,)),
                        pltpu.VMEM((tg, D), BF16),
                        pltpu.VMEM((tg, D), F32)])
    y = pl.pallas_call(
        functools.partial(_moe_group_kernel, tg=tg, sub=sub),
        grid_spec=grid_spec,
        out_shape=jax.ShapeDtypeStruct((P, D), F32),
        compiler_params=_cparams(2),
        name="moe_group",
    )(tile_e, n_sub, src, x2d, g.reshape(1, D), gate_sorted, w1, w3, w2)

    tc = _pick_tile(M, (512, 256, 128))
    pos2 = pos.reshape(M, TOP_K)
    blk = pl.BlockSpec((tc * SUBLANES, LANES), lambda i, p0, p1: (i, 0))
    out2d = pl.pallas_call(
        functools.partial(_moe_combine_kernel, tc=tc),
        grid_spec=pltpu.PrefetchScalarGridSpec(
            num_scalar_prefetch=2,
            grid=(M // tc,),
            in_specs=[blk, pl.BlockSpec(memory_space=pl.ANY)],
            out_specs=blk,
            scratch_shapes=[pltpu.VMEM((tc * SUBLANES, LANES), F32),
                            pltpu.VMEM((tc * SUBLANES, LANES), F32),
                            pltpu.SemaphoreType.DMA((2,))]),
        out_shape=jax.ShapeDtypeStruct((M * SUBLANES, LANES), F32),
        compiler_params=_cparams(1),
        name="moe_combine",
    )(pos2[:, 0], pos2[:, 1], x2d, y.reshape(P * SUBLANES, LANES))
    return out2d.reshape(M, D)


def _rope_head(xh, tc, ts1, ts2):
    return xh * tc + pltpu.roll(xh, 16, 1) * ts1 + pltpu.roll(xh, LANES - 16, 1) * ts2


def _mla_prep_kernel(cq_ref, kr_ref, ckv_ref, tc_ref, ts1_ref, ts2_ref,
                     qag_ref, kvg_ref, qng_ref, kng_ref, wuq_ref, wuk_ref, wuv_ref,
                     q_ref, k_ref, v_ref, rows_ref, *, precise):
    tc, ts1, ts2 = tc_ref[...], ts1_ref[...], ts2_ref[...]
    cq = cq_ref[...]
    cqn = cq * lax.rsqrt(jnp.mean(cq * cq, axis=-1, keepdims=True) + EPS) * qag_ref[...]
    qraw = _mm(cqn, wuq_ref[...], precise)
    ckv = ckv_ref[...]
    lat = ckv * lax.rsqrt(jnp.mean(ckv * ckv, axis=-1, keepdims=True) + EPS) * kvg_ref[...]
    kraw = _mm(lat, wuk_ref[...], precise)
    v_ref[...] = _mm(lat, wuv_ref[...], precise).astype(v_ref.dtype)
    krr = _rope_head(pltpu.roll(kr_ref[...], QK_NOPE, 1), tc, ts1, ts2)
    lane = lax.broadcasted_iota(jnp.int32, krr.shape, 1)
    krr = jnp.where(jnp.logical_and(lane >= QK_NOPE, lane < QK_HEAD), krr, 0.0)
    rows_ref[:, :KV_LORA] = lat
    rows_ref[:, KV_LORA:] = pltpu.roll(krr, LANES - QK_NOPE, 1)[:, :QK_ROPE]
    qng = qng_ref[...]
    kng = kng_ref[...]
    for h in range(MLA_HEADS):
        sl = slice(h * HEAD_PAD, (h + 1) * HEAD_PAD)
        qh = _rope_head(qraw[:, sl], tc, ts1, ts2)
        qs = jnp.sum(qh * qh, axis=-1, keepdims=True) * (1.0 / QK_HEAD)
        q_ref[:, sl] = (qh * lax.rsqrt(qs + EPS) * qng).astype(q_ref.dtype)
        kh = kraw[:, sl] + krr
        ks = jnp.sum(kh * kh, axis=-1, keepdims=True) * (1.0 / QK_HEAD)
        k_ref[:, sl] = (kh * lax.rsqrt(ks + EPS) * kng).astype(k_ref.dtype)


def mla_prep(z, tabs, qag, kvg, qng, kng, wuq, wuk, wuv, precise=False):
    M = z.shape[0]
    tm = _pick_tile(M, (512, 256, 128))
    HP = MLA_HEADS * HEAD_PAD
    odt = F32 if precise else BF16
    col = lambda off, w: pl.BlockSpec((tm, w), lambda i: (i, off // w))
    row = lambda w: pl.BlockSpec((1, w), lambda i: (0, 0))
    full = lambda a: pl.BlockSpec(a.shape, lambda i: (0, 0))
    tab = pl.BlockSpec((tm, LANES), lambda i: (i, 0))
    return pl.pallas_call(
        functools.partial(_mla_prep_kernel, precise=precise),
        grid=(M // tm,),
        in_specs=[col(Z_CQ, Q_LORA), col(Z_KR, LANES), col(Z_CKV, KV_LORA), tab, tab, tab,
                  row(Q_LORA), row(KV_LORA), row(LANES), row(LANES),
                  full(wuq), full(wuk), full(wuv)],
        out_specs=[pl.BlockSpec((tm, HP), lambda i: (i, 0)),
                   pl.BlockSpec((tm, HP), lambda i: (i, 0)),
                   pl.BlockSpec((tm, MLA_HEADS * V_HEAD), lambda i: (i, 0)),
                   pl.BlockSpec((tm, LAT_W), lambda i: (i, 0))],
        out_shape=[jax.ShapeDtypeStruct((M, HP), odt),
                   jax.ShapeDtypeStruct((M, HP), odt),
                   jax.ShapeDtypeStruct((M, MLA_HEADS * V_HEAD), odt),
                   jax.ShapeDtypeStruct((M, LAT_W), F32)],
        compiler_params=_cparams(1),
        name="mla_prep",
    )(z, z, z, *tabs, qag, kvg, qng, kng, wuq, wuk, wuv)


def _attn_prompt_kernel(q_ref, k_ref, v_ref, o_ref, *, tq):
    i = pl.program_id(1)
    rid = lax.broadcasted_iota(jnp.int32, (tq, tq), 0)
    cid = lax.broadcasted_iota(jnp.int32, (tq, tq), 1)
    causal = cid <= rid
    for h in range(MLA_HEADS):
        ksl = slice(h * HEAD_PAD, (h + 1) * HEAD_PAD)
        vsl = slice(h * V_HEAD, (h + 1) * V_HEAD)
        qh = q_ref[:, ksl]

        def step(s, vblk, carry):
            m, l, acc = carry
            m_new = jnp.maximum(m, jnp.max(s, axis=-1, keepdims=True))
            p = jnp.exp(s - m_new)
            alpha = jnp.exp(m - m_new)
            l_new = alpha * l + jnp.sum(p, axis=-1, keepdims=True)
            acc_new = alpha * acc + _dot(p.astype(BF16), vblk)
            return m_new, l_new, acc_new

        def body(j, carry):
            r0 = pl.multiple_of(j * tq, tq)
            s = _dot_nt(qh, k_ref[pl.ds(r0, tq), ksl])
            return step(s, v_ref[pl.ds(r0, tq), vsl], carry)

        init = (jnp.full((tq, 1), NEG, F32), jnp.zeros((tq, 1), F32), jnp.zeros((tq, V_HEAD), F32))
        carry = lax.fori_loop(0, i, body, init)
        r0 = pl.multiple_of(i * tq, tq)
        s = jnp.where(causal, _dot_nt(qh, k_ref[pl.ds(r0, tq), ksl]), NEG)
        m, l, acc = step(s, v_ref[pl.ds(r0, tq), vsl], carry)
        o_ref[:, vsl] = (acc / l).astype(BF16)


def attn_prompt(q, k, v, B, TP):
    tq = _pick_tile(TP, (384, 256, 128))
    nq = TP // tq
    HP = MLA_HEADS * HEAD_PAD
    VW = MLA_HEADS * V_HEAD
    return pl.pallas_call(
        functools.partial(_attn_prompt_kernel, tq=tq),
        grid=(B, nq),
        in_specs=[pl.BlockSpec((tq, HP), lambda b, i: (b * nq + i, 0)),
                  pl.BlockSpec((TP, HP), lambda b, i: (b, 0)),
                  pl.BlockSpec((TP, VW), lambda b, i: (b, 0))],
        out_specs=pl.BlockSpec((tq, VW), lambda b, i: (b * nq + i, 0)),
        out_shape=jax.ShapeDtypeStruct((B * TP, VW), BF16),
        compiler_params=_cparams(2),
        name="attn_prompt",
    )(q, k, v)


def _attn_sample_kernel(pt_ref, qn_ref, qr_ref, nrow_ref, cache_ref, wukt_ref, wuv_ref,
                        o_ref, buf_ref, sem_ref, l_ref, kr_ref, sc_ref, latb_ref,
                        *, layer, n_pages, chunk):
    b = pl.program_id(0)
    nb = pl.num_programs(0)
    ppc = chunk // PAGE
    n_chunks = n_pages // ppc
    slot = b % 2
    H = MLA_HEADS
    NK = MLA_HEADS * QK_NOPE

    def page_copy(seq, p, s):
        return pltpu.make_async_copy(cache_ref.at[layer, pt_ref[seq, p]],
                                     buf_ref.at[s, p // ppc, :, pl.ds((p % ppc) * PAGE, PAGE)],
                                     sem_ref.at[s])

    def start_all(seq, s):
        for p in range(n_pages):
            page_copy(seq, p, s).start()

    @pl.when(b == 0)
    def _():
        start_all(0, 0)
        l_ref[:NK, :] = wukt_ref[...]
        kr_ref[...] = jnp.zeros_like(kr_ref)

    @pl.when(b + 1 < nb)
    def _():
        start_all(b + 1, 1 - slot)

    l_ref[NK:, :] = _dot(qn_ref[0], wukt_ref[...]).astype(BF16)
    qr = qr_ref[0]
    hid = lax.broadcasted_iota(jnp.int32, (H, 1), 0)

    def scores(rows_t, n):
        latb = rows_t[:KV_LORA, :].astype(BF16)
        kr = rows_t[KV_LORA:, :]
        kr_ref[:QK_ROPE, :n] = kr
        kt = _dot(l_ref[...], latb)
        raw = kt[NK:NK + H, :] + _dot(qr, kr_ref[:, :n].astype(BF16))[:H]
        krsq = jnp.sum(kr * kr, axis=0, keepdims=True)
        ss = jnp.zeros((H, n), F32)
        for h in range(H):
            blk = kt[h * QK_NOPE:(h + 1) * QK_NOPE, :]
            ss = ss + jnp.where(hid == h, jnp.sum(blk * blk, axis=0, keepdims=True), 0.0)
        return raw * lax.rsqrt((ss + krsq) * (1.0 / QK_HEAD) + EPS), latb

    sc_new, latb_new = scores(nrow_ref[0], PAGE)
    sc_new = jnp.where(lax.broadcasted_iota(jnp.int32, sc_new.shape, 1) == 0, sc_new, NEG)

    for p in range(n_pages):
        page_copy(b, p, slot).wait()

    def chunk_body(c, _):
        sc, latb = scores(buf_ref[slot, c], chunk)
        sc_ref[c] = sc
        latb_ref[c] = latb
        return 0

    lax.fori_loop(0, n_chunks, chunk_body, 0)

    sc = sc_ref[...]
    mx = jnp.maximum(jnp.max(jnp.max(sc, axis=-1, keepdims=True), axis=0),
                     jnp.max(sc_new, axis=-1, keepdims=True))
    p_new = jnp.exp(sc_new - mx)
    l = jnp.sum(p_new, axis=-1, keepdims=True)
    zpad = jnp.zeros((H, PAGE), F32)
    o_lat = _dot_nt(jnp.concatenate([p_new, zpad], axis=0).astype(BF16), latb_new)[:H]
    zpad = jnp.zeros((H, chunk), F32)
    for c in range(n_chunks):
        pc = jnp.exp(sc[c] - mx)
        l = l + jnp.sum(pc, axis=-1, keepdims=True)
        o_lat = o_lat + _dot_nt(jnp.concatenate([pc, zpad], axis=0).astype(BF16), latb_ref[c])[:H]
    o_lat = o_lat / l
    full = _dot(jnp.concatenate([o_lat, jnp.zeros_like(o_lat)], axis=0).astype(BF16), wuv_ref[...])[:H]
    cid = lax.broadcasted_iota(jnp.int32, full.shape, 1)
    rid = lax.broadcasted_iota(jnp.int32, full.shape, 0)
    o_ref[0] = jnp.sum(jnp.where(cid // V_HEAD == rid, full, 0.0), axis=0, keepdims=True)


def attn_sample(qn, qr, new_rows, cache_t, layer, page_table, wukt, wuv):
    Bs, n_pages = page_table.shape
    t_past = n_pages * PAGE
    chunk = _pick_tile(t_past, (1024, 512, 256, 128))
    n_chunks = t_past // chunk
    H2 = 2 * MLA_HEADS
    NK = MLA_HEADS * QK_NOPE
    VW = MLA_HEADS * V_HEAD
    new_t = jnp.pad(new_rows[:, :, None], ((0, 0), (0, 0), (0, PAGE - 1)))
    grid_spec = pltpu.PrefetchScalarGridSpec(
        num_scalar_prefetch=1,
        grid=(Bs,),
        in_specs=[pl.BlockSpec((1, H2, NK), lambda b, pt: (b, 0, 0)),
                  pl.BlockSpec((1, H2, LANES), lambda b, pt: (b, 0, 0)),
                  pl.BlockSpec((1, LAT_W, PAGE), lambda b, pt: (b, 0, 0)),
                  pl.BlockSpec(memory_space=pl.ANY),
                  pl.BlockSpec(wukt.shape, lambda b, pt: (0, 0)),
                  pl.BlockSpec(wuv.shape, lambda b, pt: (0, 0))],
        out_specs=pl.BlockSpec((1, 1, VW), lambda b, pt: (b, 0, 0)),
        scratch_shapes=[pltpu.VMEM((2, n_chunks, LAT_W, chunk), F32),
                        pltpu.SemaphoreType.DMA((2,)),
                        pltpu.VMEM((NK + H2, KV_LORA), BF16),
                        pltpu.VMEM((LANES, chunk), F32),
                        pltpu.VMEM((n_chunks, MLA_HEADS, chunk), F32),
                        pltpu.VMEM((n_chunks, KV_LORA, chunk), BF16)])
    out = pl.pallas_call(
        functools.partial(_attn_sample_kernel, layer=layer, n_pages=n_pages, chunk=chunk),
        grid_spec=grid_spec,
        out_shape=jax.ShapeDtypeStruct((Bs, 1, VW), F32),
        compiler_params=_cparams(1),
        name="attn_sample",
    )(page_table, qn, qr, new_t, cache_t, wukt, wuv)
    return out.reshape(Bs, VW)


def _mlstm_prompt_kernel(u_ref, v_ref, og_ref, gt_ref, cw_ref, cb_ref, wq_ref, wk_ref, gb_ref, ng_ref,
                         hm_ref, c_out, n_out, m_out,
                         uprev_ref, cs_ref, ns_ref, ms_ref, *, t_valid):
    L = SEQ_BLOCK
    i = pl.program_id(1)

    @pl.when(i == 0)
    def _():
        uprev_ref[...] = jnp.zeros_like(uprev_ref)
        cs_ref[...] = jnp.zeros_like(cs_ref)
        ns_ref[...] = jnp.zeros_like(ns_ref)
        ms_ref[...] = jnp.zeros_like(ms_ref)

    u = u_ref[...]
    prev8 = uprev_ref[...]
    cw = cw_ref[...]
    uc = cb_ref[...] + cw[3:4] * u
    for k in range(1, CONV_W):
        uc = uc + cw[3 - k:4 - k] * _shift_rows(u, prev8, k)
    uprev_ref[...] = u[L - SUBLANES:, :]
    uc = uc * _sigmoid(uc)

    gb = gt_ref[...] + gb_ref[...]
    row = lax.broadcasted_iota(jnp.int32, (L, 1), 0) + i * L
    valid = row < t_valid
    li_all = jnp.where(valid, gb, NEG)
    lf_all = jnp.where(valid, _log_sigmoid(gb), 0.0)
    rid = lax.broadcasted_iota(jnp.int32, (L, L), 0)
    cid = lax.broadcasted_iota(jnp.int32, (L, L), 1)
    causal = cid <= rid
    tri = causal.astype(BF16)
    lf_hi, lf_lo = _split_bf16(lf_all)
    b_all = _dot(tri, lf_hi) + _dot(tri, lf_lo)
    li_t = li_all.T
    b_t = b_all.T

    for h in range(ML_HEADS):
        sl = slice(h * ML_HEAD, (h + 1) * ML_HEAD)
        uh = uc[:, sl].astype(BF16)
        qh = _dot(uh, wq_ref[h])
        kh = _dot(uh, wk_ref[h]) * (ML_HEAD ** -0.5)
        vh = v_ref[:, sl]
        qb, kb = qh.astype(BF16), kh.astype(BF16)
        li_col = li_all[:, h:h + 1]
        li_row = li_t[h:h + 1, :]
        b_col = b_all[:, ML_HEADS + h:ML_HEADS + h + 1]
        b_row = b_t[ML_HEADS + h:ML_HEADS + h + 1, :]
        m_prev = ms_ref[h:h + 1, 0:1]
        n_row = ns_ref[h:h + 1, :]
        c_prev = cs_ref[h]

        dm = jnp.where(causal, b_col - b_row + li_row, NEG)
        inter = b_col + m_prev
        m_t = jnp.maximum(inter, jnp.max(dm, axis=-1, keepdims=True))
        w_intra = jnp.exp(dm - m_t)
        w_inter = jnp.exp(inter - m_t)
        s = _dot_nt(qb, kb) * w_intra
        num = _dot(s.astype(BF16), vh.astype(BF16)) + w_inter * _dot_nt(qb, c_prev.astype(BF16))
        den = jnp.sum(s, axis=-1, keepdims=True) + w_inter * jnp.sum(qh * n_row, axis=-1, keepdims=True)
        hh = num / jnp.maximum(jnp.abs(den), jnp.exp(-m_t))

        b_end = b_col[L - 1:L, :]
        g_col = b_end - b_col + li_col
        g_row = b_end - b_row + li_row
        m_new = jnp.maximum(b_end + m_prev, jnp.max(g_row, axis=-1, keepdims=True))
        wg_col = jnp.exp(g_col - m_new)
        wc = jnp.exp(b_end + m_prev - m_new)
        cs_ref[h] = wc * c_prev + _dot_tn((vh * wg_col).astype(BF16), kb)
        ns_ref[h:h + 1, :] = wc * n_row + jnp.sum(kh * wg_col, axis=0, keepdims=True)
        ms_ref[h:h + 1, :] = jnp.broadcast_to(m_new, (1, LANES))

        hn = hh * lax.rsqrt(jnp.mean(hh * hh, axis=-1, keepdims=True) + EPS) * ng_ref[h:h + 1, :]
        hm_ref[:, sl] = (hn * _sigmoid(og_ref[:, sl])).astype(BF16)

    @pl.when(i == pl.num_programs(1) - 1)
    def _():
        c_out[0] = cs_ref[...]
        n_out[0] = ns_ref[:ML_HEADS, :]
        m_out[0] = ms_ref[:ML_HEADS, :]


def mlstm_prompt(z, cw, cb, wq, wk, gate_bias, ng, B, TP, t_valid):
    nc = TP // SEQ_BLOCK
    zcol = lambda off, w: pl.BlockSpec((SEQ_BLOCK, w), lambda b, i: (b * nc + i, off // w))
    full = lambda a: pl.BlockSpec(a.shape, lambda b, i: (0,) * a.ndim)
    return pl.pallas_call(
        functools.partial(_mlstm_prompt_kernel, t_valid=t_valid),
        grid=(B, nc),
        in_specs=[zcol(Z_U, ML_W), zcol(Z_V, ML_W), zcol(Z_OG, ML_W), zcol(Z_GT, LANES),
                  full(cw), full(cb), full(wq), full(wk), full(gate_bias), full(ng)],
        out_specs=[pl.BlockSpec((SEQ_BLOCK, ML_W), lambda b, i: (b * nc + i, 0)),
                   pl.BlockSpec((1, ML_HEADS, ML_HEAD, ML_HEAD), lambda b, i: (b, 0, 0, 0)),
                   pl.BlockSpec((1, ML_HEADS, ML_HEAD), lambda b, i: (b, 0, 0)),
                   pl.BlockSpec((1, ML_HEADS, LANES), lambda b, i: (b, 0, 0))],
        out_shape=[jax.ShapeDtypeStruct((B * TP, ML_W), BF16),
                   jax.ShapeDtypeStruct((B, ML_HEADS, ML_HEAD, ML_HEAD), F32),
                   jax.ShapeDtypeStruct((B, ML_HEADS, ML_HEAD), F32),
                   jax.ShapeDtypeStruct((B, ML_HEADS, LANES), F32)],
        scratch_shapes=[pltpu.VMEM((SUBLANES, ML_W), F32),
                        pltpu.VMEM((ML_HEADS, ML_HEAD, ML_HEAD), F32),
                        pltpu.VMEM((SUBLANES, ML_HEAD), F32),
                        pltpu.VMEM((SUBLANES, LANES), F32)],
        compiler_params=_cparams(2),
        name="mlstm_prompt",
    )(z, z, z, z, cw, cb, wq, wk, gate_bias, ng)


def _mlstm_sample_kernel(u_ref, v_ref, og_ref, gt_ref, buf_ref, c_ref, n_ref, m_ref,
                         cw_ref, cb_ref, wq_ref, wk_ref, gb_ref, ng_ref,
                         hm_ref, c_out, n_out, m_out, buf_out):
    u = u_ref[0]
    buf = buf_ref[0]
    cw = cw_ref[...]
    uc = cb_ref[...] + cw[3:4] * u
    for j in range(CONV_W - 1):
        uc = uc + cw[j:j + 1] * buf[j:j + 1]
    buf_out[0, 0:CONV_W - 2, :] = buf[1:CONV_W - 1]
    buf_out[0, CONV_W - 2:CONV_W - 1, :] = u
    uc = uc * _sigmoid(uc)
    gb = gt_ref[0] + gb_ref[...]
    lane = lax.broadcasted_iota(jnp.int32, (1, LANES), 1)
    m_all = m_ref[0]
    m_acc = jnp.zeros((1, LANES), F32)
    for h in range(ML_HEADS):
        sl = slice(h * ML_HEAD, (h + 1) * ML_HEAD)
        uh = jnp.broadcast_to(uc[:, sl], (ROWS16, ML_HEAD))
        qh = _dot3(uh, wq_ref[h])
        kh = _dot3(uh, wk_ref[h]) * (ML_HEAD ** -0.5)
        vh = v_ref[0][:, sl]
        li = gb[:, h:h + 1]
        lf = _log_sigmoid(gb[:, ML_HEADS + h:ML_HEADS + h + 1])
        m_prev = m_all[:, h:h + 1]
        c_prev = c_ref[0, h]
        n_row = n_ref[0, h:h + 1, :]
        m_t = jnp.maximum(lf + m_prev, li)
        w_in = jnp.exp(li - m_t)
        w_st = jnp.exp(lf + m_prev - m_t)
        s = jnp.sum(qh[0:1] * kh[0:1], axis=-1, keepdims=True) * w_in
        cq = _dot3(qh, c_prev, _dot_nt)[0:1]
        num = s * vh + w_st * cq
        den = s + w_st * jnp.sum(qh[0:1] * n_row, axis=-1, keepdims=True)
        hh = num / jnp.maximum(jnp.abs(den), jnp.exp(-m_t))
        v16 = jnp.where(lax.broadcasted_iota(jnp.int32, (ROWS16, ML_HEAD), 0) == 0,
                        jnp.broadcast_to(vh * w_in, (ROWS16, ML_HEAD)), 0.0)
        c_out[0, h] = w_st * c_prev + _dot3(v16, kh, _dot_tn)
        n_out[0, h:h + 1, :] = w_st * n_row + w_in * kh[0:1]
        m_acc = m_acc + jnp.where(lane == h, m_t, 0.0)
        hn = hh * lax.rsqrt(jnp.mean(hh * hh, axis=-1, keepdims=True) + EPS) * ng_ref[h:h + 1, :]
        hm_ref[0, :, sl] = hn * _sigmoid(og_ref[0][:, sl])
    m_out[0] = m_acc


def mlstm_sample(zs, buf, C, n, m, cw, cb, wq, wk, gate_bias, ng):
    Bs = zs.shape[0]
    mp = jnp.pad(m, ((0, 0), (0, LANES - ML_HEADS))).reshape(Bs, 1, LANES)
    zcol = lambda off, w: pl.BlockSpec((1, 1, w), lambda b: (b, 0, off // w))
    full = lambda a: pl.BlockSpec(a.shape, lambda b: (0,) * a.ndim)
    seq = lambda a: pl.BlockSpec((1,) + a.shape[1:], lambda b: (b,) + (0,) * (a.ndim - 1))
    outs = pl.pallas_call(
        _mlstm_sample_kernel,
        grid=(Bs,),
        in_specs=[zcol(Z_U, ML_W), zcol(Z_V, ML_W), zcol(Z_OG, ML_W), zcol(Z_GT, LANES),
                  seq(buf), seq(C), seq(n), seq(mp),
                  full(cw), full(cb), full(wq), full(wk), full(gate_bias), full(ng)],
        out_specs=[pl.BlockSpec((1, 1, ML_W), lambda b: (b, 0, 0)),
                   seq(C), seq(n), seq(mp), seq(buf)],
        out_shape=[jax.ShapeDtypeStruct((Bs, 1, ML_W), F32),
                   jax.ShapeDtypeStruct(C.shape, F32),
                   jax.ShapeDtypeStruct(n.shape, F32),
                   jax.ShapeDtypeStruct(mp.shape, F32),
                   jax.ShapeDtypeStruct(buf.shape, F32)],
        compiler_params=_cparams(1),
        name="mlstm_sample",
    )(zs, zs, zs, zs, buf, C, n, mp, cw, cb, wq, wk, gate_bias, ng)
    hm, C_new, n_new, m_new, buf_new = outs
    return hm.reshape(Bs, ML_W), C_new, n_new, m_new.reshape(Bs, LANES)[:, :ML_HEADS], buf_new


def _lru_gates(xc, wa_ref, wx_ref, ba, bx, c, precise=False):
    rs, iss = [], []
    for g in range(LRU_BLOCKS):
        xg = xc[:, g * LANES:(g + 1) * LANES]
        rs.append(_mm(xg, wa_ref[g], precise))
        iss.append(_mm(xg, wx_ref[g], precise))
    r = _sigmoid(jnp.concatenate(rs, axis=-1) + ba)
    ig = _sigmoid(jnp.concatenate(iss, axis=-1) + bx)
    a = jnp.exp(c * r)
    return a, jnp.sqrt(1.0 - a * a) * ig * xc


def _rglru_prompt_kernel(gate_ref, xb_ref, cw_ref, cb_ref, wa_ref, wx_ref, ba_ref, bx_ref, c_ref,
                         y_ref, h_out, xprev_ref, hs_ref, *, t_valid):
    T = SEQ_BLOCK
    i = pl.program_id(1)

    @pl.when(i == 0)
    def _():
        xprev_ref[...] = jnp.zeros_like(xprev_ref)
        hs_ref[...] = jnp.zeros_like(hs_ref)

    xb = xb_ref[...]
    prev8 = xprev_ref[...]
    cw = cw_ref[...]
    xc = cb_ref[...] + cw[3:4] * xb
    for k in range(1, CONV_W):
        xc = xc + cw[3 - k:4 - k] * _shift_rows(xb, prev8, k)
    xprev_ref[...] = xb[T - SUBLANES:, :]

    a, bx = _lru_gates(xc, wa_ref, wx_ref, ba_ref[...], bx_ref[...], c_ref[...])
    row = lax.broadcasted_iota(jnp.int32, (T, 1), 0)
    valid = row + i * T < t_valid
    a = jnp.where(valid, a, 1.0)
    bx = jnp.where(valid, bx, 0.0)
    k = 1
    while k < T:
        keep = row >= k
        a_sh = jnp.where(keep, pltpu.roll(a, k, 0), 1.0)
        b_sh = jnp.where(keep, pltpu.roll(bx, k, 0), 0.0)
        bx = a * b_sh + bx
        a = a * a_sh
        k *= 2
    hs = a * hs_ref[0:1, :] + bx
    hs_ref[...] = jnp.broadcast_to(hs[T - 1:T, :], hs_ref.shape)
    y_ref[...] = (hs * jax.nn.gelu(gate_ref[...])).astype(BF16)

    @pl.when(i == pl.num_programs(1) - 1)
    def _():
        h_out[0] = hs[T - 1:T, :]


def rglru_prompt(zz, cw, cb, wa, wx, ba, bx, c, B, TP, t_valid):
    W = zz.shape[1] // 2
    nb = TP // SEQ_BLOCK
    full = lambda a: pl.BlockSpec(a.shape, lambda b, i: (0,) * a.ndim)
    return pl.pallas_call(
        functools.partial(_rglru_prompt_kernel, t_valid=t_valid),
        grid=(B, nb),
        in_specs=[pl.BlockSpec((SEQ_BLOCK, W), lambda b, i: (b * nb + i, 0)),
                  pl.BlockSpec((SEQ_BLOCK, W), lambda b, i: (b * nb + i, 1)),
                  full(cw), full(cb), full(wa), full(wx), full(ba), full(bx), full(c)],
        out_specs=[pl.BlockSpec((SEQ_BLOCK, W), lambda b, i: (b * nb + i, 0)),
                   pl.BlockSpec((1, 1, W), lambda b, i: (b, 0, 0))],
        out_shape=[jax.ShapeDtypeStruct((B * TP, W), BF16),
                   jax.ShapeDtypeStruct((B, 1, W), F32)],
        scratch_shapes=[pltpu.VMEM((SUBLANES, W), F32), pltpu.VMEM((SUBLANES, W), F32)],
        compiler_params=_cparams(2),
        name="rglru_prompt",
    )(zz, zz, cw, cb, wa, wx, ba, bx, c)


def _rglru_sample_kernel(gate_ref, xb_ref, b0_ref, b1_ref, b2_ref, h0_ref,
                         cw_ref, cb_ref, wa_ref, wx_ref, ba_ref, bx_ref, c_ref,
                         y_ref, h_out):
    xb = xb_ref[...]
    cw = cw_ref[...]
    xc = cb_ref[...] + cw[0:1] * b0_ref[...] + cw[1:2] * b1_ref[...] + cw[2:3] * b2_ref[...] + cw[3:4] * xb
    a, bx = _lru_gates(xc, wa_ref, wx_ref, ba_ref[...], bx_ref[...], c_ref[...], precise=True)
    hs = a * h0_ref[...] + bx
    h_out[...] = hs
    y_ref[...] = hs * jax.nn.gelu(gate_ref[...])


def rglru_sample(gate, xb, buf, h0, cw, cb, wa, wx, ba, bx, c):
    Bs, W = xb.shape
    args = (gate, xb, buf[:, 0], buf[:, 1], buf[:, 2], h0, cw, cb, wa, wx, ba, bx, c)
    full = lambda a: pl.BlockSpec(a.shape, lambda i: (0,) * a.ndim)
    return pl.pallas_call(
        _rglru_sample_kernel,
        grid=(1,),
        in_specs=[full(a) for a in args],
        out_specs=[pl.BlockSpec((Bs, W), lambda i: (0, 0)), pl.BlockSpec((Bs, W), lambda i: (0, 0))],
        out_shape=[jax.ShapeDtypeStruct((Bs, W), F32), jax.ShapeDtypeStruct((Bs, W), F32)],
        compiler_params=_cparams(1),
        name="rglru_sample",
    )(*args)


def _pad_cols(w, n):
    return jnp.pad(w, ((0, 0), (0, n - w.shape[1])))


def _even_in_proj(w):
    o = 0
    pieces = {}
    for name, width in (("cq", Q_LORA), ("ckv", KV_LORA), ("kr", QK_ROPE), ("u", ML_W), ("v", ML_W),
                        ("og", ML_W), ("ig", ML_HEADS), ("fg", ML_HEADS)):
        pieces[name] = w[:, o:o + width]
        o += width
    gates = _pad_cols(jnp.concatenate([pieces["ig"], pieces["fg"]], axis=1), Z_W - Z_GT)
    return jnp.concatenate([pieces["cq"], _pad_cols(pieces["kr"], LANES), pieces["u"], pieces["v"],
                            pieces["og"], pieces["ckv"], gates], axis=1)


def _head_pad(w):
    K = w.shape[0]
    d = w.shape[1] // MLA_HEADS
    w3 = w.reshape(K, MLA_HEADS, d)
    return jnp.pad(w3, ((0, 0), (0, 0), (0, HEAD_PAD - d))).reshape(K, MLA_HEADS * HEAD_PAD)


def _rope_tables(pos):
    half = QK_ROPE // 2
    inv_freq = ROPE_THETA ** (-jnp.arange(half, dtype=F32) / half)
    ang = pos.astype(F32)[:, None] * inv_freq[None, :]
    cos, sin = jnp.cos(ang), jnp.sin(ang)
    n = pos.shape[0]
    one = jnp.ones((n, QK_NOPE), F32)
    zh = jnp.zeros((n, half), F32)
    zn = jnp.zeros((n, QK_NOPE), F32)
    zt = jnp.zeros((n, HEAD_PAD - QK_HEAD), F32)
    tc = jnp.concatenate([one, cos, cos, zt], axis=1)
    ts1 = jnp.concatenate([zn, zh, sin, zt], axis=1)
    ts2 = jnp.concatenate([zn, -sin, zh, zt], axis=1)
    return tc, ts1, ts2


def kernel(x_prompt, x_sample, cache_mla, state_mlstm_C, state_mlstm_n, state_mlstm_m, state_mlstm_conv, state_rglru_h, state_rglru_conv, page_table, meta_tokens, norm_mix_even, w_in_even, mla_q_a_g, mla_w_uq, mla_kv_a_g, mla_w_uk, mla_w_uv, mla_q_norm_g, mla_k_norm_g, ml_conv_w, ml_conv_b, ml_w_q, ml_w_k, ml_b_i, ml_b_f, ml_norm_g, w_out_even, norm_ffn_even, ffn_w1, ffn_w3, ffn_w2, norm_mix_odd, w_in_odd, lru_conv_w, lru_conv_b, lru_w_a, lru_b_a, lru_w_x, lru_b_x, lru_lam, w_out_odd, norm_ffn_odd, moe_router, moe_w1, moe_w3, moe_w2):
    B, S, D = x_prompt.shape
    Bs = x_sample.shape[0]
    assert x_sample.shape[1] == 1
    Tp = N_META + S
    TP = _round_up(Tp, SEQ_BLOCK)
    n_prompt = B * TP
    past_len = page_table.shape[1] * PAGE
    depth = norm_mix_even.shape[0] + norm_mix_odd.shape[0]

    meta = jnp.broadcast_to(meta_tokens[None], (B, N_META, D))
    xp = jnp.pad(jnp.concatenate([meta, x_prompt], axis=1), ((0, 0), (0, TP - Tp), (0, 0)))
    x = xp.reshape(n_prompt, D)
    xs = x_sample.reshape(Bs, D)
    tabs = _rope_tables(jnp.tile(jnp.arange(TP, dtype=jnp.int32), B))
    tabs_s = _rope_tables(jnp.full((Bs,), past_len, jnp.int32))
    bf = lambda a: a.astype(BF16)
    cache_t = jnp.swapaxes(cache_mla, 2, 3)

    def prompt_view(a):
        return a.reshape((B, TP) + a.shape[1:])

    rows_p, rows_s, Cp, Cs, np_, ns, mp, ms, cbp, cbs = [], [], [], [], [], [], [], [], [], []
    hp_l, hs_l, lbp, lbs = [], [], [], []
    for layer in range(depth):
        if layer % 2 == 0:
            e = layer // 2
            w_in = _even_in_proj(w_in_even[e])
            z = norm_matmul(x, norm_mix_even[e], bf(w_in))
            zs = norm_matmul(xs, norm_mix_even[e], w_in, precise=True)
            qng = jnp.pad(mla_q_norm_g[e] * (QK_HEAD ** -0.5), (0, HEAD_PAD - QK_HEAD)).reshape(1, HEAD_PAD)
            kng = jnp.pad(mla_k_norm_g[e], (0, HEAD_PAD - QK_HEAD)).reshape(1, HEAD_PAD)
            wuk2 = mla_w_uk[e].reshape(KV_LORA, MLA_HEADS * QK_NOPE)
            wuv2 = mla_w_uv[e].reshape(KV_LORA, MLA_HEADS * V_HEAD)
            prep_g = (mla_q_a_g[e].reshape(1, Q_LORA), mla_kv_a_g[e].reshape(1, KV_LORA), qng, kng)
            prep_w = (_head_pad(mla_w_uq[e]), _head_pad(wuk2), wuv2)
            q, k, v, rows = mla_prep(z, tabs, *prep_g, *(bf(w) for w in prep_w))
            q_s, _, _, rows_sample = mla_prep(zs, tabs_s, *prep_g, *prep_w, precise=True)
            att_p = attn_prompt(q, k, v, B, TP)
            qs = q_s.reshape(Bs, MLA_HEADS, HEAD_PAD) * kng.reshape(1, 1, HEAD_PAD)
            eye = jnp.eye(MLA_HEADS, dtype=F32)
            qn_bd = (qs[:, :, None, :QK_NOPE] * eye[None, :, :, None]).reshape(Bs, MLA_HEADS, MLA_HEADS * QK_NOPE)
            qr_p = jnp.pad(qs[:, :, QK_NOPE:QK_HEAD], ((0, 0), (0, 0), (0, LANES - QK_ROPE)))
            pad_heads = lambda a: bf(jnp.pad(a, ((0, 0), (0, MLA_HEADS), (0, 0))))
            att_s = attn_sample(pad_heads(qn_bd), pad_heads(qr_p), rows_sample, cache_t, e, page_table,
                                bf(wuk2.T), bf(wuv2))
            gate_bias = jnp.pad(jnp.concatenate([ml_b_i[e], ml_b_f[e]]), (0, LANES - 2 * ML_HEADS)).reshape(1, LANES)
            ml_conv = (ml_conv_w[e], ml_conv_b[e].reshape(1, ML_W))
            hm_p, c_p, n_p, m_p = mlstm_prompt(z, *ml_conv, bf(ml_w_q[e]), bf(ml_w_k[e]), gate_bias, ml_norm_g[e],
                                               B, TP, Tp)
            hm_s, c_s, n_s, m_s, cb_s = mlstm_sample(zs.reshape(Bs, 1, Z_W), state_mlstm_conv[e], state_mlstm_C[e],
                                                     state_mlstm_n[e], state_mlstm_m[e], *ml_conv,
                                                     ml_w_q[e], ml_w_k[e], gate_bias, ml_norm_g[e])
            x = matmul_residual(jnp.concatenate([att_p, hm_p], axis=1), bf(w_out_even[e]), x)
            xs = matmul_residual(jnp.concatenate([att_s, hm_s], axis=1), w_out_even[e], xs, precise=True)
            x = swiglu_residual(x, norm_ffn_even[e], bf(ffn_w1[e][None]), bf(ffn_w3[e][None]), bf(ffn_w2[e][None]))
            xs = swiglu_residual(xs, norm_ffn_even[e], ffn_w1[e][None], ffn_w3[e][None], ffn_w2[e][None],
                                 precise=True)
            rows_p.append(prompt_view(rows)[:, :Tp])
            rows_s.append(rows_sample.reshape(Bs, 1, LAT_W))
            Cp.append(c_p); np_.append(n_p); mp.append(m_p[:, :, 0])
            Cs.append(c_s); ns.append(n_s); ms.append(m_s)
            cbp.append(prompt_view(z)[:, Tp - (CONV_W - 1):Tp, Z_U:Z_U + ML_W])
            cbs.append(cb_s)
        else:
            o = layer // 2
            W = lru_conv_w.shape[2]
            zz = norm_matmul(x, norm_mix_odd[o], bf(w_in_odd[o]))
            zs = norm_matmul(xs, norm_mix_odd[o], w_in_odd[o], precise=True)
            c = (-LRU_C * jax.nn.softplus(-lru_lam[o])).reshape(1, W)
            lru_conv = (lru_conv_w[o], lru_conv_b[o].reshape(1, W))
            lru_bias = (lru_b_a[o].reshape(1, W), lru_b_x[o].reshape(1, W), c)
            y_p, h_p = rglru_prompt(zz, *lru_conv, bf(lru_w_a[o]), bf(lru_w_x[o]), *lru_bias, B, TP, Tp)
            y_s, h_s = rglru_sample(zs[:, :W], zs[:, W:], state_rglru_conv[o], state_rglru_h[o], *lru_conv,
                                    lru_w_a[o], lru_w_x[o], *lru_bias)
            x = matmul_residual(y_p, bf(w_out_odd[o]), x)
            xs = matmul_residual(y_s, w_out_odd[o], xs, precise=True)
            x = moe_sparse_residual(x, norm_ffn_odd[o], router(x, norm_ffn_odd[o], moe_router[o]),
                                    moe_w1[o], moe_w3[o], moe_w2[o])
            xs = swiglu_residual(xs, norm_ffn_odd[o], moe_w1[o], moe_w3[o], moe_w2[o],
                                 router(xs, norm_ffn_odd[o], moe_router[o]), precise=True)
            hp_l.append(h_p[:, 0]); hs_l.append(h_s)
            lbp.append(prompt_view(zz)[:, Tp - (CONV_W - 1):Tp, W:])
            lbs.append(jnp.concatenate([state_rglru_conv[o][:, 1:], zs[:, None, W:]], axis=1))
    y_prompt = prompt_view(x)[:, N_META:Tp]
    y_sample = xs.reshape(Bs, 1, D)
    return (y_prompt, y_sample,
            jnp.stack(rows_p), jnp.stack(rows_s),
            jnp.stack(Cp), jnp.stack(Cs), jnp.stack(np_), jnp.stack(ns), jnp.stack(mp), jnp.stack(ms),
            jnp.stack(cbp), jnp.stack(cbs),
            jnp.stack(hp_l), jnp.stack(hs_l), jnp.stack(lbp), jnp.stack(lbs))
```

```python
import functools

import jax
import jax.numpy as jnp
from jax import lax
from jax.experimental import pallas as pl
from jax.experimental.pallas import tpu as pltpu

F32 = jnp.float32
BF16 = jnp.bfloat16

N_META = 16
EPS = 1e-6
MLA_HEADS = 8
Q_LORA = 384
KV_LORA = 256
QK_NOPE = 64
QK_ROPE = 32
V_HEAD = 64
QK_HEAD = QK_NOPE + QK_ROPE
LAT_W = KV_LORA + QK_ROPE
ROPE_THETA = 10000.0
ML_HEADS = 4
ML_HEAD = 128
ML_W = ML_HEADS * ML_HEAD
CONV_W = 4
LRU_BLOCKS = 8
LRU_C = 8.0
TOP_K = 2
PAGE = 128

LANES = 128
SUBLANES = 8
ROWS16 = 16
HEAD_PAD = 128
SEQ_BLOCK = 128
NEG = -1e30
VMEM_LIMIT = 56 * 1024 * 1024

R_IDX, R_GATE = 8, 10
ROW_TILE = SUBLANES * LANES

Z_CQ, Z_KR, Z_U, Z_V, Z_OG, Z_CKV, Z_GT, Z_W = 0, 384, 512, 1024, 1536, 2048, 2304, 2560


def _cparams(n_axes):
    return pltpu.CompilerParams(dimension_semantics=("arbitrary",) * n_axes,
                                vmem_limit_bytes=VMEM_LIMIT)


def _round_up(a, b):
    return (a + b - 1) // b * b


def _pick_tile(n, prefs):
    for t in prefs:
        if n % t == 0:
            return t
    if n < min(prefs):
        return n
    raise ValueError(f"no tile in {prefs} divides {n}")


def _sigmoid(x):
    return 1.0 / (1.0 + jnp.exp(-x))


def _log_sigmoid(x):
    return jnp.minimum(x, 0.0) - jnp.log(1.0 + jnp.exp(-jnp.abs(x)))


def _dot(a, b):
    return jnp.dot(a, b, preferred_element_type=F32)


def _dot_nt(a, b):
    return lax.dot_general(a, b, (((1,), (1,)), ((), ())), preferred_element_type=F32)


def _dot_tn(a, b):
    return lax.dot_general(a, b, (((0,), (0,)), ((), ())), preferred_element_type=F32)


def _split_bf16(x):
    hi = x.astype(BF16)
    lo = (x - hi.astype(F32)).astype(BF16)
    return hi, lo


def _dot3(a, b, dot=_dot):
    ah, al = _split_bf16(a)
    bh, bl = _split_bf16(b)
    return dot(ah, bh) + dot(al, bh) + dot(ah, bl)


def _mm(a, w, precise):
    if precise:
        return _dot3(a.astype(F32), w)
    return _dot(a.astype(BF16), w)


def _shift_rows(x, prev8, k):
    rolled = pltpu.roll(x, k, 0)
    head = pltpu.roll(prev8, k, 0)
    rid = lax.broadcasted_iota(jnp.int32, (SUBLANES, x.shape[1]), 0)
    top = jnp.where(rid < k, head, rolled[:SUBLANES])
    if x.shape[0] == SUBLANES:
        return top
    return jnp.concatenate([top, rolled[SUBLANES:]], axis=0)


def _norm_mm_kernel(x_ref, g_ref, w_ref, o_ref, *, precise):
    x = x_ref[...]
    ms = jnp.mean(x * x, axis=-1, keepdims=True)
    xn = x * lax.rsqrt(ms + EPS) * g_ref[...]
    o_ref[...] = _mm(xn, w_ref[...], precise)


def norm_matmul(x, g, w, precise=False):
    M, K = x.shape
    N = w.shape[1]
    tm = _pick_tile(M, (512, 256, 128))
    return pl.pallas_call(
        functools.partial(_norm_mm_kernel, precise=precise),
        grid=(M // tm,),
        in_specs=[pl.BlockSpec((tm, K), lambda i: (i, 0)),
                  pl.BlockSpec((1, K), lambda i: (0, 0)),
                  pl.BlockSpec((K, N), lambda i: (0, 0))],
        out_specs=pl.BlockSpec((tm, N), lambda i: (i, 0)),
        out_shape=jax.ShapeDtypeStruct((M, N), F32),
        compiler_params=_cparams(1),
        name="norm_matmul",
    )(x, g.reshape(1, K), w)


def _mm_res_kernel(a_ref, w_ref, r_ref, o_ref, *, precise):
    o_ref[...] = r_ref[...] + _mm(a_ref[...], w_ref[...], precise)


def matmul_residual(a, w, res, precise=False):
    M, K = a.shape
    N = w.shape[1]
    tm = _pick_tile(M, (1024, 768, 512, 256, 128))
    return pl.pallas_call(
        functools.partial(_mm_res_kernel, precise=precise),
        grid=(M // tm,),
        in_specs=[pl.BlockSpec((tm, K), lambda i: (i, 0)),
                  pl.BlockSpec((K, N), lambda i: (0, 0)),
                  pl.BlockSpec((tm, N), lambda i: (i, 0))],
        out_specs=pl.BlockSpec((tm, N), lambda i: (i, 0)),
        out_shape=jax.ShapeDtypeStruct((M, N), F32),
        compiler_params=_cparams(1),
        name="matmul_residual",
    )(a, w, res)


def _ffn_kernel(*refs, gated, precise):
    if gated:
        x_ref, g_ref, comb_ref, w1_ref, w3_ref, w2_ref, o_ref, xn_ref, acc_ref = refs
    else:
        x_ref, g_ref, w1_ref, w3_ref, w2_ref, o_ref, xn_ref, acc_ref = refs
    e = pl.program_id(1)
    f = pl.program_id(2)
    first = jnp.logical_and(e == 0, f == 0)
    last = jnp.logical_and(e == pl.num_programs(1) - 1, f == pl.num_programs(2) - 1)

    @pl.when(first)
    def _():
        x = x_ref[...]
        ms = jnp.mean(x * x, axis=-1, keepdims=True)
        xn_ref[...] = (x * lax.rsqrt(ms + EPS) * g_ref[...]).astype(xn_ref.dtype)
        acc_ref[...] = jnp.zeros_like(acc_ref)

    xn = xn_ref[...]
    h1 = _mm(xn, w1_ref[0, 0], precise)
    h3 = _mm(xn, w3_ref[0, 0], precise)
    a = h1 * _sigmoid(h1) * h3
    if gated:
        comb = comb_ref[...]
        lane = lax.broadcasted_iota(jnp.int32, comb.shape, 1)
        a = a * jnp.sum(jnp.where(lane == e, comb, 0.0), axis=-1, keepdims=True)
    acc_ref[...] += _mm(a, w2_ref[0, 0], precise)

    @pl.when(last)
    def _():
        o_ref[...] = x_ref[...] + acc_ref[...]


def swiglu_residual(x, g, w1, w3, w2, layer, comb=None, precise=False):
    M, D = x.shape
    _, E, _, F = w1.shape
    tm = _pick_tile(M, (1024, 768, 512, 256, 128))
    tf = _pick_tile(F, (256, 128))
    gated = comb is not None
    in_specs = [pl.BlockSpec((tm, D), lambda i, e, f: (i, 0)),
                pl.BlockSpec((1, D), lambda i, e, f: (0, 0))]
    args = [x, g.reshape(1, D)]
    if gated:
        in_specs.append(pl.BlockSpec((tm, LANES), lambda i, e, f: (i, 0)))
        args.append(comb)
    in_specs += [pl.BlockSpec((1, 1, D, tf), lambda i, e, f: (layer, e, 0, f)),
                 pl.BlockSpec((1, 1, D, tf), lambda i, e, f: (layer, e, 0, f)),
                 pl.BlockSpec((1, 1, tf, D), lambda i, e, f: (layer, e, f, 0))]
    args += [w1, w3, w2]
    return pl.pallas_call(
        functools.partial(_ffn_kernel, gated=gated, precise=precise),
        grid=(M // tm, E, F // tf),
        in_specs=in_specs,
        out_specs=pl.BlockSpec((tm, D), lambda i, e, f: (i, 0)),
        out_shape=jax.ShapeDtypeStruct((M, D), F32),
        scratch_shapes=[pltpu.VMEM((tm, D), F32 if precise else BF16), pltpu.VMEM((tm, D), F32)],
        compiler_params=_cparams(3),
        name="swiglu_gated" if gated else "swiglu",
    )(*args)


def _router_kernel(x_ref, g_ref, w_ref, o_ref, *, n_experts):
    x = x_ref[...]
    ms = jnp.mean(x * x, axis=-1, keepdims=True)
    logits = _dot3(x * lax.rsqrt(ms + EPS) * g_ref[...], w_ref[...])
    lane = lax.broadcasted_iota(jnp.int32, logits.shape, 1)
    lg = jnp.where(lane < n_experts, logits, -jnp.inf)
    v1 = jnp.max(lg, axis=-1, keepdims=True)
    lane_f = lane.astype(F32)
    i1 = jnp.min(jnp.where(lg == v1, lane_f, float(LANES)), axis=-1, keepdims=True)
    lg2 = jnp.where(lane_f == i1, -jnp.inf, lg)
    v2 = jnp.max(lg2, axis=-1, keepdims=True)
    i2 = jnp.min(jnp.where(lg2 == v2, lane_f, float(LANES)), axis=-1, keepdims=True)
    ex = jnp.exp(v2 - v1)
    g1 = 1.0 / (1.0 + ex)
    g2 = ex / (1.0 + ex)
    comb = jnp.where(lane_f == i1, g1, 0.0) + jnp.where(lane_f == i2, g2, 0.0)
    for col, val in ((R_IDX, i1), (R_IDX + 1, i2), (R_GATE, g1), (R_GATE + 1, g2)):
        comb = jnp.where(lane == col, val, comb)
    o_ref[...] = comb


def router(x, g, w_router):
    M, D = x.shape
    E = w_router.shape[1]
    assert E <= R_IDX
    tm = _pick_tile(M, (512, 256, 128))
    return pl.pallas_call(
        functools.partial(_router_kernel, n_experts=E),
        grid=(M // tm,),
        in_specs=[pl.BlockSpec((tm, D), lambda i: (i, 0)),
                  pl.BlockSpec((1, D), lambda i: (0, 0)),
                  pl.BlockSpec((D, LANES), lambda i: (0, 0))],
        out_specs=pl.BlockSpec((tm, LANES), lambda i: (i, 0)),
        out_shape=jax.ShapeDtypeStruct((M, LANES), F32),
        compiler_params=_cparams(1),
        name="router",
    )(x, g.reshape(1, D), jnp.pad(w_router, ((0, 0), (0, LANES - E))))


def _moe_group_kernel(te_ref, ns_ref, src_ref, x2d_ref, g_ref, gate_ref, w1_ref, w3_ref, w2_ref,
                      o_ref, buf_ref, sem_ref, xn_ref, acc_ref, *, tg, sub):
    t = pl.program_id(0)
    f = pl.program_id(1)
    nt = pl.num_programs(0)
    slot = t % 2

    def issue(tile, s):
        def body(r, _):
            tok = src_ref[tile * tg + r]
            pltpu.make_async_copy(x2d_ref.at[pl.ds(pl.multiple_of(tok * SUBLANES, SUBLANES), SUBLANES), :],
                                  buf_ref.at[s, pl.ds(pl.multiple_of(r * SUBLANES, SUBLANES), SUBLANES), :],
                                  sem_ref.at[s]).start()
            return 0
        lax.fori_loop(0, tg, body, 0, unroll=4)

    @pl.when(f == 0)
    def _():
        @pl.when(t == 0)
        def _():
            issue(0, 0)

        pltpu.make_async_copy(x2d_ref.at[pl.ds(0, tg * SUBLANES), :], buf_ref.at[slot], sem_ref.at[slot]).wait()

        @pl.when(t + 1 < nt)
        def _():
            issue(t + 1, 1 - slot)

        x = jnp.concatenate([buf_ref[slot, pl.ds(k, tg, stride=SUBLANES), :] for k in range(SUBLANES)], axis=1)
        ms = jnp.mean(x * x, axis=-1, keepdims=True)
        xn_ref[...] = (x * lax.rsqrt(ms + EPS) * g_ref[...]).astype(BF16)
        acc_ref[...] = jnp.zeros_like(acc_ref)

    w1 = w1_ref[0, 0]
    w3 = w3_ref[0, 0]
    w2 = w2_ref[0, 0]
    n_sub = ns_ref[t]
    n_full = tg // sub

    def expert_rows(rows):
        xn = xn_ref[rows, :]
        h1 = _dot(xn, w1)
        h3 = _dot(xn, w3)
        acc_ref[rows, :] += _dot((h1 * _sigmoid(h1) * h3).astype(BF16), w2)

    @pl.when(n_sub == n_full)
    def _():
        expert_rows(slice(0, tg))

    for s in range(n_full - 1):
        @pl.when(jnp.logical_and(s < n_sub, n_sub < n_full))
        def _():
            expert_rows(slice(s * sub, (s + 1) * sub))

    @pl.when(f == pl.num_programs(1) - 1)
    def _():
        y = acc_ref[...] * gate_ref[...]
        for k in range(SUBLANES):
            o_ref[pl.ds(k, tg, stride=SUBLANES), :] = y[:, k * LANES:(k + 1) * LANES]


def _moe_combine_kernel(p0_ref, p1_ref, x_ref, y2d_ref, o_ref, bufa_ref, bufb_ref, sem_ref, *, tc):
    i = pl.program_id(0)

    def body(r, _):
        dst = pl.ds(pl.multiple_of(r * SUBLANES, SUBLANES), SUBLANES)
        for p_ref, buf, s in ((p0_ref, bufa_ref, 0), (p1_ref, bufb_ref, 1)):
            pos = p_ref[i * tc + r]
            pltpu.make_async_copy(y2d_ref.at[pl.ds(pl.multiple_of(pos * SUBLANES, SUBLANES), SUBLANES), :],
                                  buf.at[dst, :], sem_ref.at[s]).start()
        return 0

    lax.fori_loop(0, tc, body, 0)
    for buf, s in ((bufa_ref, 0), (bufb_ref, 1)):
        pltpu.make_async_copy(y2d_ref.at[pl.ds(0, tc * SUBLANES), :], buf, sem_ref.at[s]).wait()
    o_ref[...] = x_ref[...] + bufa_ref[...] + bufb_ref[...]


def moe_sparse_residual(x, g, comb, w1, w3, w2, layer):
    M, D = x.shape
    assert D == ROW_TILE
    _, E, _, F = w1.shape
    S = TOP_K * M
    tg = 1024 if S >= 8192 else 256
    sub = tg // 4
    tf = _pick_tile(F, (512, 256, 128))
    n_tiles = S // tg + E
    P = n_tiles * tg

    e_flat = comb[:, R_IDX:R_IDX + TOP_K].astype(jnp.int32).reshape(S)
    gates = comb[:, R_GATE:R_GATE + TOP_K].reshape(S)
    onehot = (e_flat[:, None] == jnp.arange(E, dtype=jnp.int32)[None, :]).astype(jnp.int32)
    csum = jnp.cumsum(onehot, axis=0)
    counts = csum[-1]
    rank = jnp.take_along_axis(csum, e_flat[:, None], axis=1)[:, 0] - 1
    padded = (counts + tg - 1) // tg * tg
    ends = jnp.cumsum(padded)
    starts = ends - padded
    pos = starts[e_flat] + rank
    slot_of = jnp.full((P,), S, jnp.int32).at[pos].set(jnp.arange(S, dtype=jnp.int32))
    src = jnp.where(slot_of < S, slot_of // TOP_K, 0)
    gate_sorted = jnp.concatenate([gates, jnp.zeros((1,), F32)])[slot_of].reshape(P, 1)
    tile_start = jnp.arange(n_tiles, dtype=jnp.int32) * tg
    tile_e = jnp.minimum(jnp.sum((tile_start[:, None] >= ends[None, :]).astype(jnp.int32), axis=1), E - 1)
    valid = jnp.clip(starts[tile_e] + counts[tile_e] - tile_start, 0, tg)
    n_sub = (valid + sub - 1) // sub

    x2d = x.reshape(M * SUBLANES, LANES)
    grid_spec = pltpu.PrefetchScalarGridSpec(
        num_scalar_prefetch=3,
        grid=(n_tiles, F // tf),
        in_specs=[pl.BlockSpec(memory_space=pl.ANY),
                  pl.BlockSpec((1, D), lambda t, f, te, ns, sr: (0, 0)),
                  pl.BlockSpec((tg, 1), lambda t, f, te, ns, sr: (t, 0)),
                  pl.BlockSpec((1, 1, D, tf), lambda t, f, te, ns, sr: (layer, te[t], 0, f)),
                  pl.BlockSpec((1, 1, D, tf), lambda t, f, te, ns, sr: (layer, te[t], 0, f)),
                  pl.BlockSpec((1, 1, tf, D), lambda t, f, te, ns, sr: (layer, te[t], f, 0))],
        out_specs=pl.BlockSpec((tg * SUBLANES, LANES), lambda t, f, te, ns, sr: (t, 0)),
        scratch_shapes=[pltpu.VMEM((2, tg * SUBLANES, LANES), F32),
                        pltpu.SemaphoreType.DMA((2,)),
                        pltpu.VMEM((tg, D), BF16),
                        pltpu.VMEM((tg, D), F32)])
    y2d = pl.pallas_call(
        functools.partial(_moe_group_kernel, tg=tg, sub=sub),
        grid_spec=grid_spec,
        out_shape=jax.ShapeDtypeStruct((P * SUBLANES, LANES), F32),
        compiler_params=_cparams(2),
        name="moe_group",
    )(tile_e, n_sub, src, x2d, g.reshape(1, D), gate_sorted, w1, w3, w2)

    tc = _pick_tile(M, (512, 256, 128))
    pos2 = pos.reshape(M, TOP_K)
    blk = pl.BlockSpec((tc * SUBLANES, LANES), lambda i, p0, p1: (i, 0))
    out2d = pl.pallas_call(
        functools.partial(_moe_combine_kernel, tc=tc),
        grid_spec=pltpu.PrefetchScalarGridSpec(
            num_scalar_prefetch=2,
            grid=(M // tc,),
            in_specs=[blk, pl.BlockSpec(memory_space=pl.ANY)],
            out_specs=blk,
            scratch_shapes=[pltpu.VMEM((tc * SUBLANES, LANES), F32),
                            pltpu.VMEM((tc * SUBLANES, LANES), F32),
                            pltpu.SemaphoreType.DMA((2,))]),
        out_shape=jax.ShapeDtypeStruct((M * SUBLANES, LANES), F32),
        compiler_params=_cparams(1),
        name="moe_combine",
    )(pos2[:, 0], pos2[:, 1], x2d, y2d)
    return out2d.reshape(M, D)


def _rope_head(xh, tc, ts1, ts2):
    return xh * tc + pltpu.roll(xh, 16, 1) * ts1 + pltpu.roll(xh, LANES - 16, 1) * ts2


def _mla_prep_kernel(cq_ref, kr_ref, ckv_ref, tc_ref, ts1_ref, ts2_ref,
                     qag_ref, kvg_ref, qng_ref, kng_ref, wuq_ref, wuk_ref, wuv_ref,
                     q_ref, k_ref, v_ref, rows_ref, *, precise):
    tc, ts1, ts2 = tc_ref[...], ts1_ref[...], ts2_ref[...]
    cq = cq_ref[...]
    cqn = cq * lax.rsqrt(jnp.mean(cq * cq, axis=-1, keepdims=True) + EPS) * qag_ref[...]
    qraw = _mm(cqn, wuq_ref[...], precise)
    ckv = ckv_ref[...]
    lat = ckv * lax.rsqrt(jnp.mean(ckv * ckv, axis=-1, keepdims=True) + EPS) * kvg_ref[...]
    kraw = _mm(lat, wuk_ref[...], precise)
    v_ref[...] = _mm(lat, wuv_ref[...], precise).astype(v_ref.dtype)
    krr = _rope_head(pltpu.roll(kr_ref[...], QK_NOPE, 1), tc, ts1, ts2)
    lane = lax.broadcasted_iota(jnp.int32, krr.shape, 1)
    krr = jnp.where(jnp.logical_and(lane >= QK_NOPE, lane < QK_HEAD), krr, 0.0)
    rows_ref[:, :KV_LORA] = lat
    rows_ref[:, KV_LORA:] = pltpu.roll(krr, LANES - QK_NOPE, 1)[:, :QK_ROPE]
    qng = qng_ref[...]
    kng = kng_ref[...]
    for h in range(MLA_HEADS):
        sl = slice(h * HEAD_PAD, (h + 1) * HEAD_PAD)
        qh = _rope_head(qraw[:, sl], tc, ts1, ts2)
        qs = jnp.sum(qh * qh, axis=-1, keepdims=True) * (1.0 / QK_HEAD)
        q_ref[:, sl] = (qh * lax.rsqrt(qs + EPS) * qng).astype(q_ref.dtype)
        kh = kraw[:, sl] + krr
        ks = jnp.sum(kh * kh, axis=-1, keepdims=True) * (1.0 / QK_HEAD)
        k_ref[:, sl] = (kh * lax.rsqrt(ks + EPS) * kng).astype(k_ref.dtype)


def mla_prep(z, tabs, qag, kvg, qng, kng, wuq, wuk, wuv, precise=False):
    M = z.shape[0]
    tm = _pick_tile(M, (512, 256, 128))
    HP = MLA_HEADS * HEAD_PAD
    odt = F32 if precise else BF16
    col = lambda off, w: pl.BlockSpec((tm, w), lambda i: (i, off // w))
    row = lambda w: pl.BlockSpec((1, w), lambda i: (0, 0))
    full = lambda a: pl.BlockSpec(a.shape, lambda i: (0, 0))
    tab = pl.BlockSpec((tm, LANES), lambda i: (i, 0))
    return pl.pallas_call(
        functools.partial(_mla_prep_kernel, precise=precise),
        grid=(M // tm,),
        in_specs=[col(Z_CQ, Q_LORA), col(Z_KR, LANES), col(Z_CKV, KV_LORA), tab, tab, tab,
                  row(Q_LORA), row(KV_LORA), row(LANES), row(LANES),
                  full(wuq), full(wuk), full(wuv)],
        out_specs=[pl.BlockSpec((tm, HP), lambda i: (i, 0)),
                   pl.BlockSpec((tm, HP), lambda i: (i, 0)),
                   pl.BlockSpec((tm, MLA_HEADS * V_HEAD), lambda i: (i, 0)),
                   pl.BlockSpec((tm, LAT_W), lambda i: (i, 0))],
        out_shape=[jax.ShapeDtypeStruct((M, HP), odt),
                   jax.ShapeDtypeStruct((M, HP), odt),
                   jax.ShapeDtypeStruct((M, MLA_HEADS * V_HEAD), odt),
                   jax.ShapeDtypeStruct((M, LAT_W), F32)],
        compiler_params=_cparams(1),
        name="mla_prep",
    )(z, z, z, *tabs, qag, kvg, qng, kng, wuq, wuk, wuv)


def _attn_prompt_kernel(q_ref, k_ref, v_ref, o_ref, *, tq, heads_per_loop):
    i = pl.program_id(1)
    rid = lax.broadcasted_iota(jnp.int32, (tq, tq), 0)
    cid = lax.broadcasted_iota(jnp.int32, (tq, tq), 1)
    causal = cid <= rid
    ksl = lambda h: slice(h * HEAD_PAD, (h + 1) * HEAD_PAD)
    vsl = lambda h: slice(h * V_HEAD, (h + 1) * V_HEAD)

    def step(s, vblk, carry):
        m, l, acc = carry
        m_new = jnp.maximum(m, jnp.max(s, axis=-1, keepdims=True))
        p = jnp.exp(s - m_new)
        alpha = jnp.exp(m - m_new)
        l_new = alpha * l + jnp.sum(p, axis=-1, keepdims=True)
        acc_new = alpha * acc + _dot(p.astype(BF16), vblk)
        return m_new, l_new, acc_new

    for h0 in range(0, MLA_HEADS, heads_per_loop):
        heads = tuple(range(h0, h0 + heads_per_loop))
        qs = [q_ref[:, ksl(h)] for h in heads]

        def body(j, carry, heads=heads, qs=qs):
            r0 = pl.multiple_of(j * tq, tq)
            return tuple(step(_dot_nt(qs[n], k_ref[pl.ds(r0, tq), ksl(h)]), v_ref[pl.ds(r0, tq), vsl(h)], carry[n])
                         for n, h in enumerate(heads))

        init = tuple((jnp.full((tq, 1), NEG, F32), jnp.zeros((tq, 1), F32), jnp.zeros((tq, V_HEAD), F32))
                     for _ in heads)
        carry = lax.fori_loop(0, i, body, init)
        r0 = pl.multiple_of(i * tq, tq)
        for n, h in enumerate(heads):
            s = jnp.where(causal, _dot_nt(qs[n], k_ref[pl.ds(r0, tq), ksl(h)]), NEG)
            m, l, acc = step(s, v_ref[pl.ds(r0, tq), vsl(h)], carry[n])
            o_ref[:, vsl(h)] = (acc / l).astype(BF16)


def attn_prompt(q, k, v, B, TP):
    tq = _pick_tile(TP, (384, 256, 128))
    nq = TP // tq
    HP = MLA_HEADS * HEAD_PAD
    VW = MLA_HEADS * V_HEAD
    return pl.pallas_call(
        functools.partial(_attn_prompt_kernel, tq=tq, heads_per_loop=2),
        grid=(B, nq),
        in_specs=[pl.BlockSpec((tq, HP), lambda b, i: (b * nq + i, 0)),
                  pl.BlockSpec((TP, HP), lambda b, i: (b, 0)),
                  pl.BlockSpec((TP, VW), lambda b, i: (b, 0))],
        out_specs=pl.BlockSpec((tq, VW), lambda b, i: (b * nq + i, 0)),
        out_shape=jax.ShapeDtypeStruct((B * TP, VW), BF16),
        compiler_params=_cparams(2),
        name="attn_prompt",
    )(q, k, v)


def _attn_sample_kernel(pt_ref, qn_ref, qr_ref, nrow_ref, cache_ref, wukt_ref, wuv_ref,
                        o_ref, buf_ref, sem_ref, l_ref, kr_ref, sc_ref, latb_ref,
                        *, layer, n_pages, chunk):
    b = pl.program_id(0)
    nb = pl.num_programs(0)
    ppc = chunk // PAGE
    n_chunks = n_pages // ppc
    slot = b % 2
    H = MLA_HEADS
    NK = MLA_HEADS * QK_NOPE

    def page_copy(seq, p, s):
        return pltpu.make_async_copy(cache_ref.at[layer, pt_ref[seq, p]],
                                     buf_ref.at[s, p // ppc, :, pl.ds((p % ppc) * PAGE, PAGE)],
                                     sem_ref.at[s])

    def start_all(seq, s):
        for p in range(n_pages):
            page_copy(seq, p, s).start()

    @pl.when(b == 0)
    def _():
        start_all(0, 0)
        l_ref[:NK, :] = wukt_ref[...]
        kr_ref[...] = jnp.zeros_like(kr_ref)

    @pl.when(b + 1 < nb)
    def _():
        start_all(b + 1, 1 - slot)

    l_ref[NK:, :] = _dot(qn_ref[0], wukt_ref[...]).astype(BF16)
    qr = qr_ref[0]
    hid = lax.broadcasted_iota(jnp.int32, (H, 1), 0)

    def scores(rows_t, n):
        latb = rows_t[:KV_LORA, :].astype(BF16)
        kr = rows_t[KV_LORA:, :]
        kr_ref[:QK_ROPE, :n] = kr
        kt = _dot(l_ref[...], latb)
        raw = kt[NK:NK + H, :] + _dot(qr, kr_ref[:, :n].astype(BF16))[:H]
        krsq = jnp.sum(kr * kr, axis=0, keepdims=True)
        ss = jnp.zeros((H, n), F32)
        for h in range(H):
            blk = kt[h * QK_NOPE:(h + 1) * QK_NOPE, :]
            ss = ss + jnp.where(hid == h, jnp.sum(blk * blk, axis=0, keepdims=True), 0.0)
        return raw * lax.rsqrt((ss + krsq) * (1.0 / QK_HEAD) + EPS), latb

    sc_new, latb_new = scores(nrow_ref[0], PAGE)
    sc_new = jnp.where(lax.broadcasted_iota(jnp.int32, sc_new.shape, 1) == 0, sc_new, NEG)

    for p in range(n_pages):
        page_copy(b, p, slot).wait()

    def chunk_body(c, _):
        sc, latb = scores(buf_ref[slot, c], chunk)
        sc_ref[c] = sc
        latb_ref[c] = latb
        return 0

    lax.fori_loop(0, n_chunks, chunk_body, 0)

    sc = sc_ref[...]
    mx = jnp.maximum(jnp.max(jnp.max(sc, axis=-1, keepdims=True), axis=0),
                     jnp.max(sc_new, axis=-1, keepdims=True))
    p_new = jnp.exp(sc_new - mx)
    l = jnp.sum(p_new, axis=-1, keepdims=True)
    zpad = jnp.zeros((H, PAGE), F32)
    o_lat = _dot_nt(jnp.concatenate([p_new, zpad], axis=0).astype(BF16), latb_new)[:H]
    zpad = jnp.zeros((H, chunk), F32)
    for c in range(n_chunks):
        pc = jnp.exp(sc[c] - mx)
        l = l + jnp.sum(pc, axis=-1, keepdims=True)
        o_lat = o_lat + _dot_nt(jnp.concatenate([pc, zpad], axis=0).astype(BF16), latb_ref[c])[:H]
    o_lat = o_lat / l
    full = _dot(jnp.concatenate([o_lat, jnp.zeros_like(o_lat)], axis=0).astype(BF16), wuv_ref[...])[:H]
    cid = lax.broadcasted_iota(jnp.int32, full.shape, 1)
    rid = lax.broadcasted_iota(jnp.int32, full.shape, 0)
    o_ref[0] = jnp.sum(jnp.where(cid // V_HEAD == rid, full, 0.0), axis=0, keepdims=True)


def attn_sample(qn, qr, new_rows, cache_t, layer, page_table, wukt, wuv):
    Bs, n_pages = page_table.shape
    t_past = n_pages * PAGE
    chunk = _pick_tile(t_past, (1024, 512, 256, 128))
    n_chunks = t_past // chunk
    H2 = 2 * MLA_HEADS
    NK = MLA_HEADS * QK_NOPE
    VW = MLA_HEADS * V_HEAD
    new_t = jnp.pad(new_rows[:, :, None], ((0, 0), (0, 0), (0, PAGE - 1)))
    grid_spec = pltpu.PrefetchScalarGridSpec(
        num_scalar_prefetch=1,
        grid=(Bs,),
        in_specs=[pl.BlockSpec((1, H2, NK), lambda b, pt: (b, 0, 0)),
                  pl.BlockSpec((1, H2, LANES), lambda b, pt: (b, 0, 0)),
                  pl.BlockSpec((1, LAT_W, PAGE), lambda b, pt: (b, 0, 0)),
                  pl.BlockSpec(memory_space=pl.ANY),
                  pl.BlockSpec(wukt.shape, lambda b, pt: (0, 0)),
                  pl.BlockSpec(wuv.shape, lambda b, pt: (0, 0))],
        out_specs=pl.BlockSpec((1, 1, VW), lambda b, pt: (b, 0, 0)),
        scratch_shapes=[pltpu.VMEM((2, n_chunks, LAT_W, chunk), F32),
                        pltpu.SemaphoreType.DMA((2,)),
                        pltpu.VMEM((NK + H2, KV_LORA), BF16),
                        pltpu.VMEM((LANES, chunk), F32),
                        pltpu.VMEM((n_chunks, MLA_HEADS, chunk), F32),
                        pltpu.VMEM((n_chunks, KV_LORA, chunk), BF16)])
    out = pl.pallas_call(
        functools.partial(_attn_sample_kernel, layer=layer, n_pages=n_pages, chunk=chunk),
        grid_spec=grid_spec,
        out_shape=jax.ShapeDtypeStruct((Bs, 1, VW), F32),
        compiler_params=_cparams(1),
        name="attn_sample",
    )(page_table, qn, qr, new_t, cache_t, wukt, wuv)
    return out.reshape(Bs, VW)


def _mlstm_prompt_kernel(u_ref, v_ref, og_ref, gt_ref, cw_ref, cb_ref, wq_ref, wk_ref, gb_ref, ng_ref,
                         hm_ref, c_out, n_out, m_out,
                         uprev_ref, cs_ref, ns_ref, ms_ref, *, t_valid):
    L = SEQ_BLOCK
    i = pl.program_id(1)

    @pl.when(i == 0)
    def _():
        uprev_ref[...] = jnp.zeros_like(uprev_ref)
        cs_ref[...] = jnp.zeros_like(cs_ref)
        ns_ref[...] = jnp.zeros_like(ns_ref)
        ms_ref[...] = jnp.zeros_like(ms_ref)

    u = u_ref[...]
    prev8 = uprev_ref[...]
    cw = cw_ref[...]
    uc = cb_ref[...] + cw[3:4] * u
    for k in range(1, CONV_W):
        uc = uc + cw[3 - k:4 - k] * _shift_rows(u, prev8, k)
    uprev_ref[...] = u[L - SUBLANES:, :]
    uc = uc * _sigmoid(uc)

    gb = gt_ref[...] + gb_ref[...]
    row = lax.broadcasted_iota(jnp.int32, (L, 1), 0) + i * L
    valid = row < t_valid
    li_all = jnp.where(valid, gb, NEG)
    lf_all = jnp.where(valid, _log_sigmoid(gb), 0.0)
    rid = lax.broadcasted_iota(jnp.int32, (L, L), 0)
    cid = lax.broadcasted_iota(jnp.int32, (L, L), 1)
    causal = cid <= rid
    tri = causal.astype(BF16)
    lf_hi, lf_lo = _split_bf16(lf_all)
    b_all = _dot(tri, lf_hi) + _dot(tri, lf_lo)
    li_t = li_all.T
    b_t = b_all.T

    for h in range(ML_HEADS):
        sl = slice(h * ML_HEAD, (h + 1) * ML_HEAD)
        uh = uc[:, sl].astype(BF16)
        qh = _dot(uh, wq_ref[h])
        kh = _dot(uh, wk_ref[h]) * (ML_HEAD ** -0.5)
        vh = v_ref[:, sl]
        qb, kb = qh.astype(BF16), kh.astype(BF16)
        li_col = li_all[:, h:h + 1]
        li_row = li_t[h:h + 1, :]
        b_col = b_all[:, ML_HEADS + h:ML_HEADS + h + 1]
        b_row = b_t[ML_HEADS + h:ML_HEADS + h + 1, :]
        m_prev = ms_ref[h:h + 1, 0:1]
        n_row = ns_ref[h:h + 1, :]
        c_prev = cs_ref[h]

        dm = jnp.where(causal, b_col - b_row + li_row, NEG)
        inter = b_col + m_prev
        m_t = jnp.maximum(inter, jnp.max(dm, axis=-1, keepdims=True))
        w_intra = jnp.exp(dm - m_t)
        w_inter = jnp.exp(inter - m_t)
        s = _dot_nt(qb, kb) * w_intra
        num = _dot(s.astype(BF16), vh.astype(BF16)) + w_inter * _dot_nt(qb, c_prev.astype(BF16))
        den = jnp.sum(s, axis=-1, keepdims=True) + w_inter * jnp.sum(qh * n_row, axis=-1, keepdims=True)
        hh = num / jnp.maximum(jnp.abs(den), jnp.exp(-m_t))

        b_end = b_col[L - 1:L, :]
        g_col = b_end - b_col + li_col
        g_row = b_end - b_row + li_row
        m_new = jnp.maximum(b_end + m_prev, jnp.max(g_row, axis=-1, keepdims=True))
        wg_col = jnp.exp(g_col - m_new)
        wc = jnp.exp(b_end + m_prev - m_new)
        cs_ref[h] = wc * c_prev + _dot_tn((vh * wg_col).astype(BF16), kb)
        ns_ref[h:h + 1, :] = wc * n_row + jnp.sum(kh * wg_col, axis=0, keepdims=True)
        ms_ref[h:h + 1, :] = jnp.broadcast_to(m_new, (1, LANES))

        hn = hh * lax.rsqrt(jnp.mean(hh * hh, axis=-1, keepdims=True) + EPS) * ng_ref[h:h + 1, :]
        hm_ref[:, sl] = (hn * _sigmoid(og_ref[:, sl])).astype(BF16)

    @pl.when(i == pl.num_programs(1) - 1)
    def _():
        c_out[0] = cs_ref[...]
        n_out[0] = ns_ref[:ML_HEADS, :]
        m_out[0] = ms_ref[:ML_HEADS, :]


def mlstm_prompt(z, cw, cb, wq, wk, gate_bias, ng, B, TP, t_valid):
    nc = TP // SEQ_BLOCK
    zcol = lambda off, w: pl.BlockSpec((SEQ_BLOCK, w), lambda b, i: (b * nc + i, off // w))
    full = lambda a: pl.BlockSpec(a.shape, lambda b, i: (0,) * a.ndim)
    return pl.pallas_call(
        functools.partial(_mlstm_prompt_kernel, t_valid=t_valid),
        grid=(B, nc),
        in_specs=[zcol(Z_U, ML_W), zcol(Z_V, ML_W), zcol(Z_OG, ML_W), zcol(Z_GT, LANES),
                  full(cw), full(cb), full(wq), full(wk), full(gate_bias), full(ng)],
        out_specs=[pl.BlockSpec((SEQ_BLOCK, ML_W), lambda b, i: (b * nc + i, 0)),
                   pl.BlockSpec((1, ML_HEADS, ML_HEAD, ML_HEAD), lambda b, i: (b, 0, 0, 0)),
                   pl.BlockSpec((1, ML_HEADS, ML_HEAD), lambda b, i: (b, 0, 0)),
                   pl.BlockSpec((1, ML_HEADS, LANES), lambda b, i: (b, 0, 0))],
        out_shape=[jax.ShapeDtypeStruct((B * TP, ML_W), BF16),
                   jax.ShapeDtypeStruct((B, ML_HEADS, ML_HEAD, ML_HEAD), F32),
                   jax.ShapeDtypeStruct((B, ML_HEADS, ML_HEAD), F32),
                   jax.ShapeDtypeStruct((B, ML_HEADS, LANES), F32)],
        scratch_shapes=[pltpu.VMEM((SUBLANES, ML_W), F32),
                        pltpu.VMEM((ML_HEADS, ML_HEAD, ML_HEAD), F32),
                        pltpu.VMEM((SUBLANES, ML_HEAD), F32),
                        pltpu.VMEM((SUBLANES, LANES), F32)],
        compiler_params=_cparams(2),
        name="mlstm_prompt",
    )(z, z, z, z, cw, cb, wq, wk, gate_bias, ng)


def _mlstm_sample_kernel(u_ref, v_ref, og_ref, gt_ref, buf_ref, c_ref, n_ref, m_ref,
                         cw_ref, cb_ref, wq_ref, wk_ref, gb_ref, ng_ref,
                         hm_ref, c_out, n_out, m_out, buf_out):
    u = u_ref[0]
    buf = buf_ref[0]
    cw = cw_ref[...]
    uc = cb_ref[...] + cw[3:4] * u
    for j in range(CONV_W - 1):
        uc = uc + cw[j:j + 1] * buf[j:j + 1]
    buf_out[0, 0:CONV_W - 2, :] = buf[1:CONV_W - 1]
    buf_out[0, CONV_W - 2:CONV_W - 1, :] = u
    uc = uc * _sigmoid(uc)
    gb = gt_ref[0] + gb_ref[...]
    lane = lax.broadcasted_iota(jnp.int32, (1, LANES), 1)
    m_all = m_ref[0]
    m_acc = jnp.zeros((1, LANES), F32)
    for h in range(ML_HEADS):
        sl = slice(h * ML_HEAD, (h + 1) * ML_HEAD)
        uh = jnp.broadcast_to(uc[:, sl], (ROWS16, ML_HEAD))
        qh = _dot3(uh, wq_ref[h])
        kh = _dot3(uh, wk_ref[h]) * (ML_HEAD ** -0.5)
        vh = v_ref[0][:, sl]
        li = gb[:, h:h + 1]
        lf = _log_sigmoid(gb[:, ML_HEADS + h:ML_HEADS + h + 1])
        m_prev = m_all[:, h:h + 1]
        c_prev = c_ref[0, h]
        n_row = n_ref[0, h:h + 1, :]
        m_t = jnp.maximum(lf + m_prev, li)
        w_in = jnp.exp(li - m_t)
        w_st = jnp.exp(lf + m_prev - m_t)
        s = jnp.sum(qh[0:1] * kh[0:1], axis=-1, keepdims=True) * w_in
        cq = _dot3(qh, c_prev, _dot_nt)[0:1]
        num = s * vh + w_st * cq
        den = s + w_st * jnp.sum(qh[0:1] * n_row, axis=-1, keepdims=True)
        hh = num / jnp.maximum(jnp.abs(den), jnp.exp(-m_t))
        v16 = jnp.where(lax.broadcasted_iota(jnp.int32, (ROWS16, ML_HEAD), 0) == 0,
                        jnp.broadcast_to(vh * w_in, (ROWS16, ML_HEAD)), 0.0)
        c_out[0, h] = w_st * c_prev + _dot3(v16, kh, _dot_tn)
        n_out[0, h:h + 1, :] = w_st * n_row + w_in * kh[0:1]
        m_acc = m_acc + jnp.where(lane == h, m_t, 0.0)
        hn = hh * lax.rsqrt(jnp.mean(hh * hh, axis=-1, keepdims=True) + EPS) * ng_ref[h:h + 1, :]
        hm_ref[0, :, sl] = hn * _sigmoid(og_ref[0][:, sl])
    m_out[0] = m_acc


def mlstm_sample(zs, buf, C, n, m, cw, cb, wq, wk, gate_bias, ng):
    Bs = zs.shape[0]
    mp = jnp.pad(m, ((0, 0), (0, LANES - ML_HEADS))).reshape(Bs, 1, LANES)
    zcol = lambda off, w: pl.BlockSpec((1, 1, w), lambda b: (b, 0, off // w))
    full = lambda a: pl.BlockSpec(a.shape, lambda b: (0,) * a.ndim)
    seq = lambda a: pl.BlockSpec((1,) + a.shape[1:], lambda b: (b,) + (0,) * (a.ndim - 1))
    outs = pl.pallas_call(
        _mlstm_sample_kernel,
        grid=(Bs,),
        in_specs=[zcol(Z_U, ML_W), zcol(Z_V, ML_W), zcol(Z_OG, ML_W), zcol(Z_GT, LANES),
                  seq(buf), seq(C), seq(n), seq(mp),
                  full(cw), full(cb), full(wq), full(wk), full(gate_bias), full(ng)],
        out_specs=[pl.BlockSpec((1, 1, ML_W), lambda b: (b, 0, 0)),
                   seq(C), seq(n), seq(mp), seq(buf)],
        out_shape=[jax.ShapeDtypeStruct((Bs, 1, ML_W), F32),
                   jax.ShapeDtypeStruct(C.shape, F32),
                   jax.ShapeDtypeStruct(n.shape, F32),
                   jax.ShapeDtypeStruct(mp.shape, F32),
                   jax.ShapeDtypeStruct(buf.shape, F32)],
        compiler_params=_cparams(1),
        name="mlstm_sample",
    )(zs, zs, zs, zs, buf, C, n, mp, cw, cb, wq, wk, gate_bias, ng)
    hm, C_new, n_new, m_new, buf_new = outs
    return hm.reshape(Bs, ML_W), C_new, n_new, m_new.reshape(Bs, LANES)[:, :ML_HEADS], buf_new


def _lru_gates(xc, wa_ref, wx_ref, ba, bx, c, precise=False):
    rs, iss = [], []
    for g in range(LRU_BLOCKS):
        xg = xc[:, g * LANES:(g + 1) * LANES]
        rs.append(_mm(xg, wa_ref[g], precise))
        iss.append(_mm(xg, wx_ref[g], precise))
    r = _sigmoid(jnp.concatenate(rs, axis=-1) + ba)
    ig = _sigmoid(jnp.concatenate(iss, axis=-1) + bx)
    a = jnp.exp(c * r)
    return a, jnp.sqrt(1.0 - a * a) * ig * xc


def _rglru_prompt_kernel(gate_ref, xb_ref, cw_ref, cb_ref, wa_ref, wx_ref, ba_ref, bx_ref, c_ref,
                         y_ref, h_out, xprev_ref, hs_ref, *, t_valid):
    T = SEQ_BLOCK
    i = pl.program_id(1)

    @pl.when(i == 0)
    def _():
        xprev_ref[...] = jnp.zeros_like(xprev_ref)
        hs_ref[...] = jnp.zeros_like(hs_ref)

    xb = xb_ref[...]
    prev8 = xprev_ref[...]
    cw = cw_ref[...]
    xc = cb_ref[...] + cw[3:4] * xb
    for k in range(1, CONV_W):
        xc = xc + cw[3 - k:4 - k] * _shift_rows(xb, prev8, k)
    xprev_ref[...] = xb[T - SUBLANES:, :]

    a, bx = _lru_gates(xc, wa_ref, wx_ref, ba_ref[...], bx_ref[...], c_ref[...])
    row = lax.broadcasted_iota(jnp.int32, (T, 1), 0)
    valid = row + i * T < t_valid
    a = jnp.where(valid, a, 1.0)
    bx = jnp.where(valid, bx, 0.0)
    k = 1
    while k < T:
        keep = row >= k
        a_sh = jnp.where(keep, pltpu.roll(a, k, 0), 1.0)
        b_sh = jnp.where(keep, pltpu.roll(bx, k, 0), 0.0)
        bx = a * b_sh + bx
        a = a * a_sh
        k *= 2
    hs = a * hs_ref[0:1, :] + bx
    hs_ref[...] = jnp.broadcast_to(hs[T - 1:T, :], hs_ref.shape)
    y_ref[...] = (hs * jax.nn.gelu(gate_ref[...])).astype(BF16)

    @pl.when(i == pl.num_programs(1) - 1)
    def _():
        h_out[0] = hs[T - 1:T, :]


def rglru_prompt(zz, cw, cb, wa, wx, ba, bx, c, B, TP, t_valid):
    W = zz.shape[1] // 2
    nb = TP // SEQ_BLOCK
    full = lambda a: pl.BlockSpec(a.shape, lambda b, i: (0,) * a.ndim)
    return pl.pallas_call(
        functools.partial(_rglru_prompt_kernel, t_valid=t_valid),
        grid=(B, nb),
        in_specs=[pl.BlockSpec((SEQ_BLOCK, W), lambda b, i: (b * nb + i, 0)),
                  pl.BlockSpec((SEQ_BLOCK, W), lambda b, i: (b * nb + i, 1)),
                  full(cw), full(cb), full(wa), full(wx), full(ba), full(bx), full(c)],
        out_specs=[pl.BlockSpec((SEQ_BLOCK, W), lambda b, i: (b * nb + i, 0)),
                   pl.BlockSpec((1, 1, W), lambda b, i: (b, 0, 0))],
        out_shape=[jax.ShapeDtypeStruct((B * TP, W), BF16),
                   jax.ShapeDtypeStruct((B, 1, W), F32)],
        scratch_shapes=[pltpu.VMEM((SUBLANES, W), F32), pltpu.VMEM((SUBLANES, W), F32)],
        compiler_params=_cparams(2),
        name="rglru_prompt",
    )(zz, zz, cw, cb, wa, wx, ba, bx, c)


def _rglru_sample_kernel(gate_ref, xb_ref, b0_ref, b1_ref, b2_ref, h0_ref,
                         cw_ref, cb_ref, wa_ref, wx_ref, ba_ref, bx_ref, c_ref,
                         y_ref, h_out):
    xb = xb_ref[...]
    cw = cw_ref[...]
    xc = cb_ref[...] + cw[0:1] * b0_ref[...] + cw[1:2] * b1_ref[...] + cw[2:3] * b2_ref[...] + cw[3:4] * xb
    a, bx = _lru_gates(xc, wa_ref, wx_ref, ba_ref[...], bx_ref[...], c_ref[...], precise=True)
    hs = a * h0_ref[...] + bx
    h_out[...] = hs
    y_ref[...] = hs * jax.nn.gelu(gate_ref[...])


def rglru_sample(gate, xb, buf, h0, cw, cb, wa, wx, ba, bx, c):
    Bs, W = xb.shape
    args = (gate, xb, buf[:, 0], buf[:, 1], buf[:, 2], h0, cw, cb, wa, wx, ba, bx, c)
    full = lambda a: pl.BlockSpec(a.shape, lambda i: (0,) * a.ndim)
    return pl.pallas_call(
        _rglru_sample_kernel,
        grid=(1,),
        in_specs=[full(a) for a in args],
        out_specs=[pl.BlockSpec((Bs, W), lambda i: (0, 0)), pl.BlockSpec((Bs, W), lambda i: (0, 0))],
        out_shape=[jax.ShapeDtypeStruct((Bs, W), F32), jax.ShapeDtypeStruct((Bs, W), F32)],
        compiler_params=_cparams(1),
        name="rglru_sample",
    )(*args)


def _pad_cols(w, n):
    return jnp.pad(w, ((0, 0), (0, n - w.shape[1])))


def _even_in_proj(w):
    o = 0
    pieces = {}
    for name, width in (("cq", Q_LORA), ("ckv", KV_LORA), ("kr", QK_ROPE), ("u", ML_W), ("v", ML_W),
                        ("og", ML_W), ("ig", ML_HEADS), ("fg", ML_HEADS)):
        pieces[name] = w[:, o:o + width]
        o += width
    gates = _pad_cols(jnp.concatenate([pieces["ig"], pieces["fg"]], axis=1), Z_W - Z_GT)
    return jnp.concatenate([pieces["cq"], _pad_cols(pieces["kr"], LANES), pieces["u"], pieces["v"],
                            pieces["og"], pieces["ckv"], gates], axis=1)


def _head_pad(w):
    K = w.shape[0]
    d = w.shape[1] // MLA_HEADS
    w3 = w.reshape(K, MLA_HEADS, d)
    return jnp.pad(w3, ((0, 0), (0, 0), (0, HEAD_PAD - d))).reshape(K, MLA_HEADS * HEAD_PAD)


def _rope_tables(pos):
    half = QK_ROPE // 2
    inv_freq = ROPE_THETA ** (-jnp.arange(half, dtype=F32) / half)
    ang = pos.astype(F32)[:, None] * inv_freq[None, :]
    cos, sin = jnp.cos(ang), jnp.sin(ang)
    n = pos.shape[0]
    one = jnp.ones((n, QK_NOPE), F32)
    zh = jnp.zeros((n, half), F32)
    zn = jnp.zeros((n, QK_NOPE), F32)
    zt = jnp.zeros((n, HEAD_PAD - QK_HEAD), F32)
    tc = jnp.concatenate([one, cos, cos, zt], axis=1)
    ts1 = jnp.concatenate([zn, zh, sin, zt], axis=1)
    ts2 = jnp.concatenate([zn, -sin, zh, zt], axis=1)
    return tc, ts1, ts2


def kernel(x_prompt, x_sample, cache_mla, state_mlstm_C, state_mlstm_n, state_mlstm_m, state_mlstm_conv, state_rglru_h, state_rglru_conv, page_table, meta_tokens, norm_mix_even, w_in_even, mla_q_a_g, mla_w_uq, mla_kv_a_g, mla_w_uk, mla_w_uv, mla_q_norm_g, mla_k_norm_g, ml_conv_w, ml_conv_b, ml_w_q, ml_w_k, ml_b_i, ml_b_f, ml_norm_g, w_out_even, norm_ffn_even, ffn_w1, ffn_w3, ffn_w2, norm_mix_odd, w_in_odd, lru_conv_w, lru_conv_b, lru_w_a, lru_b_a, lru_w_x, lru_b_x, lru_lam, w_out_odd, norm_ffn_odd, moe_router, moe_w1, moe_w3, moe_w2):
    B, S, D = x_prompt.shape
    Bs = x_sample.shape[0]
    assert x_sample.shape[1] == 1
    Tp = N_META + S
    TP = _round_up(Tp, SEQ_BLOCK)
    n_prompt = B * TP
    past_len = page_table.shape[1] * PAGE
    depth = norm_mix_even.shape[0] + norm_mix_odd.shape[0]

    meta = jnp.broadcast_to(meta_tokens[None], (B, N_META, D))
    xp = jnp.pad(jnp.concatenate([meta, x_prompt], axis=1), ((0, 0), (0, TP - Tp), (0, 0)))
    x = xp.reshape(n_prompt, D)
    xs = x_sample.reshape(Bs, D)
    tabs = _rope_tables(jnp.tile(jnp.arange(TP, dtype=jnp.int32), B))
    tabs_s = _rope_tables(jnp.full((Bs,), past_len, jnp.int32))
    bf = lambda a: a.astype(BF16)
    cache_t = jnp.swapaxes(cache_mla, 2, 3)
    ffn_w = tuple(w[:, None] for w in (ffn_w1, ffn_w3, ffn_w2))
    ffn_bf = tuple(bf(w) for w in ffn_w)
    moe_bf = tuple(bf(w) for w in (moe_w1, moe_w3, moe_w2))

    def prompt_view(a):
        return a.reshape((B, TP) + a.shape[1:])

    rows_p, rows_s, Cp, Cs, np_, ns, mp, ms, cbp, cbs = [], [], [], [], [], [], [], [], [], []
    hp_l, hs_l, lbp, lbs = [], [], [], []
    for layer in range(depth):
        if layer % 2 == 0:
            e = layer // 2
            w_in = _even_in_proj(w_in_even[e])
            z = norm_matmul(x, norm_mix_even[e], bf(w_in))
            zs = norm_matmul(xs, norm_mix_even[e], w_in, precise=True)
            qng = jnp.pad(mla_q_norm_g[e] * (QK_HEAD ** -0.5), (0, HEAD_PAD - QK_HEAD)).reshape(1, HEAD_PAD)
            kng = jnp.pad(mla_k_norm_g[e], (0, HEAD_PAD - QK_HEAD)).reshape(1, HEAD_PAD)
            wuk2 = mla_w_uk[e].reshape(KV_LORA, MLA_HEADS * QK_NOPE)
            wuv2 = mla_w_uv[e].reshape(KV_LORA, MLA_HEADS * V_HEAD)
            prep_g = (mla_q_a_g[e].reshape(1, Q_LORA), mla_kv_a_g[e].reshape(1, KV_LORA), qng, kng)
            prep_w = (_head_pad(mla_w_uq[e]), _head_pad(wuk2), wuv2)
            q, k, v, rows = mla_prep(z, tabs, *prep_g, *(bf(w) for w in prep_w))
            q_s, _, _, rows_sample = mla_prep(zs, tabs_s, *prep_g, *prep_w, precise=True)
            att_p = attn_prompt(q, k, v, B, TP)
            qs = q_s.reshape(Bs, MLA_HEADS, HEAD_PAD) * kng.reshape(1, 1, HEAD_PAD)
            eye = jnp.eye(MLA_HEADS, dtype=F32)
            qn_bd = (qs[:, :, None, :QK_NOPE] * eye[None, :, :, None]).reshape(Bs, MLA_HEADS, MLA_HEADS * QK_NOPE)
            qr_p = jnp.pad(qs[:, :, QK_NOPE:QK_HEAD], ((0, 0), (0, 0), (0, LANES - QK_ROPE)))
            pad_heads = lambda a: bf(jnp.pad(a, ((0, 0), (0, MLA_HEADS), (0, 0))))
            att_s = attn_sample(pad_heads(qn_bd), pad_heads(qr_p), rows_sample, cache_t, e, page_table,
                                bf(wuk2.T), bf(wuv2))
            gate_bias = jnp.pad(jnp.concatenate([ml_b_i[e], ml_b_f[e]]), (0, LANES - 2 * ML_HEADS)).reshape(1, LANES)
            ml_conv = (ml_conv_w[e], ml_conv_b[e].reshape(1, ML_W))
            hm_p, c_p, n_p, m_p = mlstm_prompt(z, *ml_conv, bf(ml_w_q[e]), bf(ml_w_k[e]), gate_bias, ml_norm_g[e],
                                               B, TP, Tp)
            hm_s, c_s, n_s, m_s, cb_s = mlstm_sample(zs.reshape(Bs, 1, Z_W), state_mlstm_conv[e], state_mlstm_C[e],
                                                     state_mlstm_n[e], state_mlstm_m[e], *ml_conv,
                                                     ml_w_q[e], ml_w_k[e], gate_bias, ml_norm_g[e])
            x = matmul_residual(jnp.concatenate([att_p, hm_p], axis=1), bf(w_out_even[e]), x)
            xs = matmul_residual(jnp.concatenate([att_s, hm_s], axis=1), w_out_even[e], xs, precise=True)
            x = swiglu_residual(x, norm_ffn_even[e], *ffn_bf, e)
            xs = swiglu_residual(xs, norm_ffn_even[e], *ffn_w, e, precise=True)
            rows_p.append(prompt_view(rows)[:, :Tp])
            rows_s.append(rows_sample.reshape(Bs, 1, LAT_W))
            Cp.append(c_p); np_.append(n_p); mp.append(m_p[:, :, 0])
            Cs.append(c_s); ns.append(n_s); ms.append(m_s)
            cbp.append(prompt_view(z)[:, Tp - (CONV_W - 1):Tp, Z_U:Z_U + ML_W])
            cbs.append(cb_s)
        else:
            o = layer // 2
            W = lru_conv_w.shape[2]
            zz = norm_matmul(x, norm_mix_odd[o], bf(w_in_odd[o]))
            zs = norm_matmul(xs, norm_mix_odd[o], w_in_odd[o], precise=True)
            c = (-LRU_C * jax.nn.softplus(-lru_lam[o])).reshape(1, W)
            lru_conv = (lru_conv_w[o], lru_conv_b[o].reshape(1, W))
            lru_bias = (lru_b_a[o].reshape(1, W), lru_b_x[o].reshape(1, W), c)
            y_p, h_p = rglru_prompt(zz, *lru_conv, bf(lru_w_a[o]), bf(lru_w_x[o]), *lru_bias, B, TP, Tp)
            y_s, h_s = rglru_sample(zs[:, :W], zs[:, W:], state_rglru_conv[o], state_rglru_h[o], *lru_conv,
                                    lru_w_a[o], lru_w_x[o], *lru_bias)
            x = matmul_residual(y_p, bf(w_out_odd[o]), x)
            xs = matmul_residual(y_s, w_out_odd[o], xs, precise=True)
            x = moe_sparse_residual(x, norm_ffn_odd[o], router(x, norm_ffn_odd[o], moe_router[o]),
                                    *moe_bf, o)
            xs = swiglu_residual(xs, norm_ffn_odd[o], moe_w1, moe_w3, moe_w2, o,
                                 router(xs, norm_ffn_odd[o], moe_router[o]), precise=True)
            hp_l.append(h_p[:, 0]); hs_l.append(h_s)
            lbp.append(prompt_view(zz)[:, Tp - (CONV_W - 1):Tp, W:])
            lbs.append(jnp.concatenate([state_rglru_conv[o][:, 1:], zs[:, None, W:]], axis=1))
    y_prompt = prompt_view(x)[:, N_META:Tp]
    y_sample = xs.reshape(Bs, 1, D)
    return (y_prompt, y_sample,
            jnp.stack(rows_p), jnp.stack(rows_s),
            jnp.stack(Cp), jnp.stack(Cs), jnp.stack(np_), jnp.stack(ns), jnp.stack(mp), jnp.stack(ms),
            jnp.stack(cbp), jnp.stack(cbs),
            jnp.stack(hp_l), jnp.stack(hs_l), jnp.stack(lbp), jnp.stack(lbs))
```

```python
import functools

import jax
import jax.numpy as jnp
from jax import lax
from jax.experimental import pallas as pl
from jax.experimental.pallas import tpu as pltpu

F32 = jnp.float32
BF16 = jnp.bfloat16

N_META = 16
EPS = 1e-6
MLA_HEADS = 8
Q_LORA = 384
KV_LORA = 256
QK_NOPE = 64
QK_ROPE = 32
V_HEAD = 64
QK_HEAD = QK_NOPE + QK_ROPE
LAT_W = KV_LORA + QK_ROPE
ROPE_THETA = 10000.0
ML_HEADS = 4
ML_HEAD = 128
ML_W = ML_HEADS * ML_HEAD
CONV_W = 4
LRU_BLOCKS = 8
LRU_C = 8.0
TOP_K = 2
PAGE = 128

LANES = 128
SUBLANES = 8
ROWS16 = 16
HEAD_PAD = 128
SEQ_BLOCK = 128
NEG = -1e30
VMEM_LIMIT = 56 * 1024 * 1024

R_IDX, R_GATE = 8, 10
ROW_TILE = SUBLANES * LANES

Z_CQ, Z_KR, Z_U, Z_V, Z_OG, Z_CKV, Z_GT, Z_W = 0, 384, 512, 1024, 1536, 2048, 2304, 2560


def _cparams(n_axes):
    return pltpu.CompilerParams(dimension_semantics=("arbitrary",) * n_axes,
                                vmem_limit_bytes=VMEM_LIMIT)


def _round_up(a, b):
    return (a + b - 1) // b * b


def _pick_tile(n, prefs):
    for t in prefs:
        if n % t == 0:
            return t
    if n < min(prefs):
        return n
    raise ValueError(f"no tile in {prefs} divides {n}")


def _sigmoid(x):
    return 1.0 / (1.0 + jnp.exp(-x))


def _log_sigmoid(x):
    return jnp.minimum(x, 0.0) - jnp.log(1.0 + jnp.exp(-jnp.abs(x)))


def _dot(a, b):
    return jnp.dot(a, b, preferred_element_type=F32)


def _dot_nt(a, b):
    return lax.dot_general(a, b, (((1,), (1,)), ((), ())), preferred_element_type=F32)


def _dot_tn(a, b):
    return lax.dot_general(a, b, (((0,), (0,)), ((), ())), preferred_element_type=F32)


def _split_bf16(x):
    hi = x.astype(BF16)
    lo = (x - hi.astype(F32)).astype(BF16)
    return hi, lo


def _dot3(a, b, dot=_dot):
    ah, al = _split_bf16(a)
    bh, bl = _split_bf16(b)
    return dot(ah, bh) + dot(al, bh) + dot(ah, bl)


def _mm(a, w, precise):
    if precise:
        return _dot3(a.astype(F32), w)
    return _dot(a.astype(BF16), w)


def _shift_rows(x, prev8, k):
    rolled = pltpu.roll(x, k, 0)
    head = pltpu.roll(prev8, k, 0)
    rid = lax.broadcasted_iota(jnp.int32, (SUBLANES, x.shape[1]), 0)
    top = jnp.where(rid < k, head, rolled[:SUBLANES])
    if x.shape[0] == SUBLANES:
        return top
    return jnp.concatenate([top, rolled[SUBLANES:]], axis=0)


def _norm_mm_kernel(x_ref, g_ref, w_ref, o_ref, *, precise):
    x = x_ref[...]
    ms = jnp.mean(x * x, axis=-1, keepdims=True)
    xn = x * lax.rsqrt(ms + EPS) * g_ref[...]
    o_ref[...] = _mm(xn, w_ref[...], precise)


def norm_matmul(x, g, w, precise=False):
    M, K = x.shape
    N = w.shape[1]
    tm = _pick_tile(M, (512, 256, 128))
    return pl.pallas_call(
        functools.partial(_norm_mm_kernel, precise=precise),
        grid=(M // tm,),
        in_specs=[pl.BlockSpec((tm, K), lambda i: (i, 0)),
                  pl.BlockSpec((1, K), lambda i: (0, 0)),
                  pl.BlockSpec((K, N), lambda i: (0, 0))],
        out_specs=pl.BlockSpec((tm, N), lambda i: (i, 0)),
        out_shape=jax.ShapeDtypeStruct((M, N), F32),
        compiler_params=_cparams(1),
        name="norm_matmul",
    )(x, g.reshape(1, K), w)


def _mm_res_kernel(a_ref, w_ref, r_ref, o_ref, *tile_ref, precise):
    o = r_ref[...] + _mm(a_ref[...], w_ref[...], precise)
    o_ref[...] = o
    if tile_ref:
        _store_row_tiles(tile_ref[0], o)


def _store_row_tiles(ref, rows):
    n = rows.shape[0]
    for k in range(SUBLANES):
        ref[pl.ds(k, n, stride=SUBLANES), :] = rows[:, k * LANES:(k + 1) * LANES]


def _load_row_tiles(ref, n):
    return jnp.concatenate([ref[pl.ds(k, n, stride=SUBLANES), :] for k in range(SUBLANES)], axis=1)


def matmul_residual(a, w, res, precise=False, row_tiles=False):
    M, K = a.shape
    N = w.shape[1]
    tm = _pick_tile(M, (1024, 768, 512, 256, 128))
    out_specs = [pl.BlockSpec((tm, N), lambda i: (i, 0))]
    out_shape = [jax.ShapeDtypeStruct((M, N), F32)]
    if row_tiles:
        assert N == ROW_TILE
        out_specs.append(pl.BlockSpec((tm * SUBLANES, LANES), lambda i: (i, 0)))
        out_shape.append(jax.ShapeDtypeStruct((M * SUBLANES, LANES), F32))
    out = pl.pallas_call(
        functools.partial(_mm_res_kernel, precise=precise),
        grid=(M // tm,),
        in_specs=[pl.BlockSpec((tm, K), lambda i: (i, 0)),
                  pl.BlockSpec((K, N), lambda i: (0, 0)),
                  pl.BlockSpec((tm, N), lambda i: (i, 0))],
        out_specs=out_specs,
        out_shape=out_shape,
        compiler_params=_cparams(1),
        name="matmul_residual",
    )(a, w, res)
    return out if row_tiles else out[0]


def _ffn_kernel(*refs, gated, precise):
    if gated:
        x_ref, g_ref, comb_ref, w1_ref, w3_ref, w2_ref, o_ref, xn_ref, acc_ref = refs
    else:
        x_ref, g_ref, w1_ref, w3_ref, w2_ref, o_ref, xn_ref, acc_ref = refs
    e = pl.program_id(1)
    f = pl.program_id(2)
    first = jnp.logical_and(e == 0, f == 0)
    last = jnp.logical_and(e == pl.num_programs(1) - 1, f == pl.num_programs(2) - 1)

    @pl.when(first)
    def _():
        x = x_ref[...]
        ms = jnp.mean(x * x, axis=-1, keepdims=True)
        xn_ref[...] = (x * lax.rsqrt(ms + EPS) * g_ref[...]).astype(xn_ref.dtype)
        acc_ref[...] = jnp.zeros_like(acc_ref)

    xn = xn_ref[...]
    h1 = _mm(xn, w1_ref[0, 0], precise)
    h3 = _mm(xn, w3_ref[0, 0], precise)
    a = h1 * _sigmoid(h1) * h3
    if gated:
        comb = comb_ref[...]
        lane = lax.broadcasted_iota(jnp.int32, comb.shape, 1)
        a = a * jnp.sum(jnp.where(lane == e, comb, 0.0), axis=-1, keepdims=True)
    acc_ref[...] += _mm(a, w2_ref[0, 0], precise)

    @pl.when(last)
    def _():
        o_ref[...] = x_ref[...] + acc_ref[...]


def swiglu_residual(x, g, w1, w3, w2, layer, comb=None, precise=False):
    M, D = x.shape
    _, E, _, F = w1.shape
    tm = _pick_tile(M, (1536, 1024, 768, 512, 256, 128))
    tf = _pick_tile(F, (256, 128))
    gated = comb is not None
    in_specs = [pl.BlockSpec((tm, D), lambda i, e, f: (i, 0)),
                pl.BlockSpec((1, D), lambda i, e, f: (0, 0))]
    args = [x, g.reshape(1, D)]
    if gated:
        in_specs.append(pl.BlockSpec((tm, LANES), lambda i, e, f: (i, 0)))
        args.append(comb)
    in_specs += [pl.BlockSpec((1, 1, D, tf), lambda i, e, f: (layer, e, 0, f)),
                 pl.BlockSpec((1, 1, D, tf), lambda i, e, f: (layer, e, 0, f)),
                 pl.BlockSpec((1, 1, tf, D), lambda i, e, f: (layer, e, f, 0))]
    args += [w1, w3, w2]
    return pl.pallas_call(
        functools.partial(_ffn_kernel, gated=gated, precise=precise),
        grid=(M // tm, E, F // tf),
        in_specs=in_specs,
        out_specs=pl.BlockSpec((tm, D), lambda i, e, f: (i, 0)),
        out_shape=jax.ShapeDtypeStruct((M, D), F32),
        scratch_shapes=[pltpu.VMEM((tm, D), F32 if precise else BF16), pltpu.VMEM((tm, D), F32)],
        compiler_params=_cparams(3),
        name="swiglu_gated" if gated else "swiglu",
    )(*args)


def _router_kernel(x_ref, g_ref, w_ref, o_ref, *, n_experts):
    x = x_ref[...]
    ms = jnp.mean(x * x, axis=-1, keepdims=True)
    logits = _dot3(x * lax.rsqrt(ms + EPS) * g_ref[...], w_ref[...])
    lane = lax.broadcasted_iota(jnp.int32, logits.shape, 1)
    lg = jnp.where(lane < n_experts, logits, -jnp.inf)
    v1 = jnp.max(lg, axis=-1, keepdims=True)
    lane_f = lane.astype(F32)
    i1 = jnp.min(jnp.where(lg == v1, lane_f, float(LANES)), axis=-1, keepdims=True)
    lg2 = jnp.where(lane_f == i1, -jnp.inf, lg)
    v2 = jnp.max(lg2, axis=-1, keepdims=True)
    i2 = jnp.min(jnp.where(lg2 == v2, lane_f, float(LANES)), axis=-1, keepdims=True)
    ex = jnp.exp(v2 - v1)
    g1 = 1.0 / (1.0 + ex)
    g2 = ex / (1.0 + ex)
    comb = jnp.where(lane_f == i1, g1, 0.0) + jnp.where(lane_f == i2, g2, 0.0)
    for col, val in ((R_IDX, i1), (R_IDX + 1, i2), (R_GATE, g1), (R_GATE + 1, g2)):
        comb = jnp.where(lane == col, val, comb)
    o_ref[...] = comb


def router(x, g, w_router):
    M, D = x.shape
    E = w_router.shape[1]
    assert E <= R_IDX
    tm = _pick_tile(M, (512, 256, 128))
    return pl.pallas_call(
        functools.partial(_router_kernel, n_experts=E),
        grid=(M // tm,),
        in_specs=[pl.BlockSpec((tm, D), lambda i: (i, 0)),
                  pl.BlockSpec((1, D), lambda i: (0, 0)),
                  pl.BlockSpec((D, LANES), lambda i: (0, 0))],
        out_specs=pl.BlockSpec((tm, LANES), lambda i: (i, 0)),
        out_shape=jax.ShapeDtypeStruct((M, LANES), F32),
        compiler_params=_cparams(1),
        name="router",
    )(x, g.reshape(1, D), jnp.pad(w_router, ((0, 0), (0, LANES - E))))


def _moe_group_kernel(te_ref, ns_ref, src_ref, x2d_ref, g_ref, gate_ref, w1_ref, w3_ref, w2_ref,
                      o_ref, buf_ref, sem_ref, xn_ref, acc_ref, *, tg, sub):
    t = pl.program_id(0)
    f = pl.program_id(1)
    nt = pl.num_programs(0)
    slot = t % 2

    def issue(tile, s):
        def body(r, _):
            tok = src_ref[tile * tg + r]
            pltpu.make_async_copy(x2d_ref.at[pl.ds(pl.multiple_of(tok * SUBLANES, SUBLANES), SUBLANES), :],
                                  buf_ref.at[s, pl.ds(pl.multiple_of(r * SUBLANES, SUBLANES), SUBLANES), :],
                                  sem_ref.at[s]).start()
            return 0
        lax.fori_loop(0, tg, body, 0, unroll=4)

    @pl.when(f == 0)
    def _():
        @pl.when(t == 0)
        def _():
            issue(0, 0)

        pltpu.make_async_copy(x2d_ref.at[pl.ds(0, tg * SUBLANES), :], buf_ref.at[slot], sem_ref.at[slot]).wait()

        @pl.when(t + 1 < nt)
        def _():
            issue(t + 1, 1 - slot)

        x = _load_row_tiles(buf_ref.at[slot], tg)
        ms = jnp.mean(x * x, axis=-1, keepdims=True)
        xn_ref[...] = (x * lax.rsqrt(ms + EPS) * g_ref[...]).astype(BF16)
        acc_ref[...] = jnp.zeros_like(acc_ref)

    w1 = w1_ref[0, 0]
    w3 = w3_ref[0, 0]
    w2 = w2_ref[0, 0]
    n_sub = ns_ref[t]
    n_full = tg // sub

    def expert_rows(rows):
        xn = xn_ref[rows, :]
        h1 = _dot(xn, w1)
        h3 = _dot(xn, w3)
        acc_ref[rows, :] += _dot((h1 * _sigmoid(h1) * h3).astype(BF16), w2)

    @pl.when(n_sub == n_full)
    def _():
        expert_rows(slice(0, tg))

    for s in range(n_full - 1):
        @pl.when(jnp.logical_and(s < n_sub, n_sub < n_full))
        def _():
            expert_rows(slice(s * sub, (s + 1) * sub))

    @pl.when(f == pl.num_programs(1) - 1)
    def _():
        _store_row_tiles(o_ref, acc_ref[...] * gate_ref[...])


def _moe_combine_kernel(p0_ref, p1_ref, x_ref, y2d_ref, o_ref, bufa_ref, bufb_ref, sem_ref, *, tc):
    i = pl.program_id(0)

    def body(r, _):
        dst = pl.ds(pl.multiple_of(r * SUBLANES, SUBLANES), SUBLANES)
        for p_ref, buf, s in ((p0_ref, bufa_ref, 0), (p1_ref, bufb_ref, 1)):
            pos = p_ref[i * tc + r]
            pltpu.make_async_copy(y2d_ref.at[pl.ds(pl.multiple_of(pos * SUBLANES, SUBLANES), SUBLANES), :],
                                  buf.at[dst, :], sem_ref.at[s]).start()
        return 0

    lax.fori_loop(0, tc, body, 0)
    for buf, s in ((bufa_ref, 0), (bufb_ref, 1)):
        pltpu.make_async_copy(y2d_ref.at[pl.ds(0, tc * SUBLANES), :], buf, sem_ref.at[s]).wait()
    bufa_ref[...] = bufa_ref[...] + bufb_ref[...]
    o_ref[...] = x_ref[...] + _load_row_tiles(bufa_ref, tc)


def moe_sparse_residual(x, x2d, g, comb, w1, w3, w2, layer):
    M, D = x.shape
    assert D == ROW_TILE
    _, E, _, F = w1.shape
    S = TOP_K * M
    tg = 1024 if S >= 8192 else 256
    sub = tg // 4
    tf = _pick_tile(F, (512, 256, 128))
    n_tiles = S // tg + E
    P = n_tiles * tg

    e_flat = comb[:, R_IDX:R_IDX + TOP_K].astype(jnp.int32).reshape(S)
    gates = comb[:, R_GATE:R_GATE + TOP_K].reshape(S)
    onehot = (e_flat[:, None] == jnp.arange(E, dtype=jnp.int32)[None, :]).astype(jnp.int32)
    csum = jnp.cumsum(onehot, axis=0)
    counts = csum[-1]
    rank = jnp.take_along_axis(csum, e_flat[:, None], axis=1)[:, 0] - 1
    padded = (counts + tg - 1) // tg * tg
    ends = jnp.cumsum(padded)
    starts = ends - padded
    pos = starts[e_flat] + rank
    slot_of = jnp.full((P,), S, jnp.int32).at[pos].set(jnp.arange(S, dtype=jnp.int32))
    src = jnp.where(slot_of < S, slot_of // TOP_K, 0)
    gate_sorted = jnp.concatenate([gates, jnp.zeros((1,), F32)])[slot_of].reshape(P, 1)
    tile_start = jnp.arange(n_tiles, dtype=jnp.int32) * tg
    tile_e = jnp.minimum(jnp.sum((tile_start[:, None] >= ends[None, :]).astype(jnp.int32), axis=1), E - 1)
    valid = jnp.clip(starts[tile_e] + counts[tile_e] - tile_start, 0, tg)
    n_sub = (valid + sub - 1) // sub

    grid_spec = pltpu.PrefetchScalarGridSpec(
        num_scalar_prefetch=3,
        grid=(n_tiles, F // tf),
        in_specs=[pl.BlockSpec(memory_space=pl.ANY),
                  pl.BlockSpec((1, D), lambda t, f, te, ns, sr: (0, 0)),
                  pl.BlockSpec((tg, 1), lambda t, f, te, ns, sr: (t, 0)),
                  pl.BlockSpec((1, 1, D, tf), lambda t, f, te, ns, sr: (layer, te[t], 0, f)),
                  pl.BlockSpec((1, 1, D, tf), lambda t, f, te, ns, sr: (layer, te[t], 0, f)),
                  pl.BlockSpec((1, 1, tf, D), lambda t, f, te, ns, sr: (layer, te[t], f, 0))],
        out_specs=pl.BlockSpec((tg * SUBLANES, LANES), lambda t, f, te, ns, sr: (t, 0)),
        scratch_shapes=[pltpu.VMEM((2, tg * SUBLANES, LANES), F32),
                        pltpu.SemaphoreType.DMA((2,)),
                        pltpu.VMEM((tg, D), BF16),
                        pltpu.VMEM((tg, D), F32)])
    y2d = pl.pallas_call(
        functools.partial(_moe_group_kernel, tg=tg, sub=sub),
        grid_spec=grid_spec,
        out_shape=jax.ShapeDtypeStruct((P * SUBLANES, LANES), F32),
        compiler_params=_cparams(2),
        name="moe_group",
    )(tile_e, n_sub, src, x2d, g.reshape(1, D), gate_sorted, w1, w3, w2)

    tc = _pick_tile(M, (512, 256, 128))
    pos2 = pos.reshape(M, TOP_K)
    blk = pl.BlockSpec((tc, D), lambda i, p0, p1: (i, 0))
    return pl.pallas_call(
        functools.partial(_moe_combine_kernel, tc=tc),
        grid_spec=pltpu.PrefetchScalarGridSpec(
            num_scalar_prefetch=2,
            grid=(M // tc,),
            in_specs=[blk, pl.BlockSpec(memory_space=pl.ANY)],
            out_specs=blk,
            scratch_shapes=[pltpu.VMEM((tc * SUBLANES, LANES), F32),
                            pltpu.VMEM((tc * SUBLANES, LANES), F32),
                            pltpu.SemaphoreType.DMA((2,))]),
        out_shape=jax.ShapeDtypeStruct((M, D), F32),
        compiler_params=_cparams(1),
        name="moe_combine",
    )(pos2[:, 0], pos2[:, 1], x, y2d)


def _rope_head(xh, tc, ts1, ts2):
    return xh * tc + pltpu.roll(xh, 16, 1) * ts1 + pltpu.roll(xh, LANES - 16, 1) * ts2


def _mla_prep_kernel(cq_ref, kr_ref, ckv_ref, tc_ref, ts1_ref, ts2_ref,
                     qag_ref, kvg_ref, qng_ref, kng_ref, wuq_ref, wuk_ref, wuv_ref,
                     q_ref, k_ref, v_ref, rows_ref, *, precise):
    tc, ts1, ts2 = tc_ref[...], ts1_ref[...], ts2_ref[...]
    cq = cq_ref[...]
    cqn = cq * lax.rsqrt(jnp.mean(cq * cq, axis=-1, keepdims=True) + EPS) * qag_ref[...]
    qraw = _mm(cqn, wuq_ref[...], precise)
    ckv = ckv_ref[...]
    lat = ckv * lax.rsqrt(jnp.mean(ckv * ckv, axis=-1, keepdims=True) + EPS) * kvg_ref[...]
    kraw = _mm(lat, wuk_ref[...], precise)
    v_ref[...] = _mm(lat, wuv_ref[...], precise).astype(v_ref.dtype)
    krr = _rope_head(pltpu.roll(kr_ref[...], QK_NOPE, 1), tc, ts1, ts2)
    lane = lax.broadcasted_iota(jnp.int32, krr.shape, 1)
    krr = jnp.where(jnp.logical_and(lane >= QK_NOPE, lane < QK_HEAD), krr, 0.0)
    rows_ref[:, :KV_LORA] = lat
    rows_ref[:, KV_LORA:] = pltpu.roll(krr, LANES - QK_NOPE, 1)[:, :QK_ROPE]
    qng = qng_ref[...]
    kng = kng_ref[...]
    for h in range(MLA_HEADS):
        sl = slice(h * HEAD_PAD, (h + 1) * HEAD_PAD)
        qh = _rope_head(qraw[:, sl], tc, ts1, ts2)
        qs = jnp.sum(qh * qh, axis=-1, keepdims=True) * (1.0 / QK_HEAD)
        q_ref[:, sl] = (qh * lax.rsqrt(qs + EPS) * qng).astype(q_ref.dtype)
        kh = kraw[:, sl] + krr
        ks = jnp.sum(kh * kh, axis=-1, keepdims=True) * (1.0 / QK_HEAD)
        k_ref[:, sl] = (kh * lax.rsqrt(ks + EPS) * kng).astype(k_ref.dtype)


def mla_prep(z, tabs, qag, kvg, qng, kng, wuq, wuk, wuv, precise=False):
    M = z.shape[0]
    tm = _pick_tile(M, (512, 256, 128))
    HP = MLA_HEADS * HEAD_PAD
    odt = F32 if precise else BF16
    col = lambda off, w: pl.BlockSpec((tm, w), lambda i: (i, off // w))
    row = lambda w: pl.BlockSpec((1, w), lambda i: (0, 0))
    full = lambda a: pl.BlockSpec(a.shape, lambda i: (0, 0))
    tab = pl.BlockSpec((tm, LANES), lambda i: (i, 0))
    return pl.pallas_call(
        functools.partial(_mla_prep_kernel, precise=precise),
        grid=(M // tm,),
        in_specs=[col(Z_CQ, Q_LORA), col(Z_KR, LANES), col(Z_CKV, KV_LORA), tab, tab, tab,
                  row(Q_LORA), row(KV_LORA), row(LANES), row(LANES),
                  full(wuq), full(wuk), full(wuv)],
        out_specs=[pl.BlockSpec((tm, HP), lambda i: (i, 0)),
                   pl.BlockSpec((tm, HP), lambda i: (i, 0)),
                   pl.BlockSpec((tm, MLA_HEADS * V_HEAD), lambda i: (i, 0)),
                   pl.BlockSpec((tm, LAT_W), lambda i: (i, 0))],
        out_shape=[jax.ShapeDtypeStruct((M, HP), odt),
                   jax.ShapeDtypeStruct((M, HP), odt),
                   jax.ShapeDtypeStruct((M, MLA_HEADS * V_HEAD), odt),
                   jax.ShapeDtypeStruct((M, LAT_W), F32)],
        compiler_params=_cparams(1),
        name="mla_prep",
    )(z, z, z, *tabs, qag, kvg, qng, kng, wuq, wuk, wuv)


def _attn_prompt_kernel(q_ref, k_ref, v_ref, o_ref, *, tq, heads_per_loop):
    i = pl.program_id(1)
    rid = lax.broadcasted_iota(jnp.int32, (tq, tq), 0)
    cid = lax.broadcasted_iota(jnp.int32, (tq, tq), 1)
    causal = cid <= rid
    ksl = lambda h: slice(h * HEAD_PAD, (h + 1) * HEAD_PAD)
    vsl = lambda h: slice(h * V_HEAD, (h + 1) * V_HEAD)

    def step(s, vblk, carry):
        m, l, acc = carry
        m_new = jnp.maximum(m, jnp.max(s, axis=-1, keepdims=True))
        p = jnp.exp(s - m_new)
        alpha = jnp.exp(m - m_new)
        l_new = alpha * l + jnp.sum(p, axis=-1, keepdims=True)
        acc_new = alpha * acc + _dot(p.astype(BF16), vblk)
        return m_new, l_new, acc_new

    for h0 in range(0, MLA_HEADS, heads_per_loop):
        heads = tuple(range(h0, h0 + heads_per_loop))
        qs = [q_ref[:, ksl(h)] for h in heads]

        def body(j, carry, heads=heads, qs=qs):
            r0 = pl.multiple_of(j * tq, tq)
            return tuple(step(_dot_nt(qs[n], k_ref[pl.ds(r0, tq), ksl(h)]), v_ref[pl.ds(r0, tq), vsl(h)], carry[n])
                         for n, h in enumerate(heads))

        init = tuple((jnp.full((tq, 1), NEG, F32), jnp.zeros((tq, 1), F32), jnp.zeros((tq, V_HEAD), F32))
                     for _ in heads)
        carry = lax.fori_loop(0, i, body, init)
        r0 = pl.multiple_of(i * tq, tq)
        for n, h in enumerate(heads):
            s = jnp.where(causal, _dot_nt(qs[n], k_ref[pl.ds(r0, tq), ksl(h)]), NEG)
            m, l, acc = step(s, v_ref[pl.ds(r0, tq), vsl(h)], carry[n])
            o_ref[:, vsl(h)] = (acc / l).astype(BF16)


def attn_prompt(q, k, v, B, TP):
    tq = _pick_tile(TP, (384, 256, 128))
    nq = TP // tq
    HP = MLA_HEADS * HEAD_PAD
    VW = MLA_HEADS * V_HEAD
    return pl.pallas_call(
        functools.partial(_attn_prompt_kernel, tq=tq, heads_per_loop=4),
        grid=(B, nq),
        in_specs=[pl.BlockSpec((tq, HP), lambda b, i: (b * nq + i, 0)),
                  pl.BlockSpec((TP, HP), lambda b, i: (b, 0)),
                  pl.BlockSpec((TP, VW), lambda b, i: (b, 0))],
        out_specs=pl.BlockSpec((tq, VW), lambda b, i: (b * nq + i, 0)),
        out_shape=jax.ShapeDtypeStruct((B * TP, VW), BF16),
        compiler_params=_cparams(2),
        name="attn_prompt",
    )(q, k, v)


def _attn_sample_kernel(pt_ref, qn_ref, qr_ref, nrow_ref, cache_ref, wukt_ref, wuv_ref,
                        o_ref, buf_ref, sem_ref, l_ref, kr_ref, sc_ref, latb_ref,
                        *, layer, n_pages, chunk):
    b = pl.program_id(0)
    nb = pl.num_programs(0)
    ppc = chunk // PAGE
    n_chunks = n_pages // ppc
    slot = b % 2
    H = MLA_HEADS
    NK = MLA_HEADS * QK_NOPE

    def page_copy(seq, p, s):
        return pltpu.make_async_copy(cache_ref.at[layer, pt_ref[seq, p]],
                                     buf_ref.at[s, p // ppc, :, pl.ds((p % ppc) * PAGE, PAGE)],
                                     sem_ref.at[s])

    def start_all(seq, s):
        for p in range(n_pages):
            page_copy(seq, p, s).start()

    @pl.when(b == 0)
    def _():
        start_all(0, 0)
        l_ref[:NK, :] = wukt_ref[...]
        kr_ref[...] = jnp.zeros_like(kr_ref)

    @pl.when(b + 1 < nb)
    def _():
        start_all(b + 1, 1 - slot)

    l_ref[NK:, :] = _dot(qn_ref[0], wukt_ref[...]).astype(BF16)
    qr = qr_ref[0]
    hid = lax.broadcasted_iota(jnp.int32, (H, 1), 0)

    def scores(rows_t, n):
        latb = rows_t[:KV_LORA, :].astype(BF16)
        kr = rows_t[KV_LORA:, :]
        kr_ref[:QK_ROPE, :n] = kr
        kt = _dot(l_ref[...], latb)
        raw = kt[NK:NK + H, :] + _dot(qr, kr_ref[:, :n].astype(BF16))[:H]
        krsq = jnp.sum(kr * kr, axis=0, keepdims=True)
        ss = jnp.zeros((H, n), F32)
        for h in range(H):
            blk = kt[h * QK_NOPE:(h + 1) * QK_NOPE, :]
            ss = ss + jnp.where(hid == h, jnp.sum(blk * blk, axis=0, keepdims=True), 0.0)
        return raw * lax.rsqrt((ss + krsq) * (1.0 / QK_HEAD) + EPS), latb

    sc_new, latb_new = scores(nrow_ref[0], PAGE)
    sc_new = jnp.where(lax.broadcasted_iota(jnp.int32, sc_new.shape, 1) == 0, sc_new, NEG)

    for p in range(n_pages):
        page_copy(b, p, slot).wait()

    def chunk_body(c, _):
        sc, latb = scores(buf_ref[slot, c], chunk)
        sc_ref[c] = sc
        latb_ref[c] = latb
        return 0

    lax.fori_loop(0, n_chunks, chunk_body, 0)

    sc = sc_ref[...]
    mx = jnp.maximum(jnp.max(jnp.max(sc, axis=-1, keepdims=True), axis=0),
                     jnp.max(sc_new, axis=-1, keepdims=True))
    p_new = jnp.exp(sc_new - mx)
    l = jnp.sum(p_new, axis=-1, keepdims=True)
    zpad = jnp.zeros((H, PAGE), F32)
    o_lat = _dot_nt(jnp.concatenate([p_new, zpad], axis=0).astype(BF16), latb_new)[:H]
    zpad = jnp.zeros((H, chunk), F32)
    for c in range(n_chunks):
        pc = jnp.exp(sc[c] - mx)
        l = l + jnp.sum(pc, axis=-1, keepdims=True)
        o_lat = o_lat + _dot_nt(jnp.concatenate([pc, zpad], axis=0).astype(BF16), latb_ref[c])[:H]
    o_lat = o_lat / l
    full = _dot(jnp.concatenate([o_lat, jnp.zeros_like(o_lat)], axis=0).astype(BF16), wuv_ref[...])[:H]
    cid = lax.broadcasted_iota(jnp.int32, full.shape, 1)
    rid = lax.broadcasted_iota(jnp.int32, full.shape, 0)
    o_ref[0] = jnp.sum(jnp.where(cid // V_HEAD == rid, full, 0.0), axis=0, keepdims=True)


def attn_sample(qn, qr, new_rows, cache_t, layer, page_table, wukt, wuv):
    Bs, n_pages = page_table.shape
    t_past = n_pages * PAGE
    chunk = _pick_tile(t_past, (1024, 512, 256, 128))
    n_chunks = t_past // chunk
    H2 = 2 * MLA_HEADS
    NK = MLA_HEADS * QK_NOPE
    VW = MLA_HEADS * V_HEAD
    new_t = jnp.pad(new_rows[:, :, None], ((0, 0), (0, 0), (0, PAGE - 1)))
    grid_spec = pltpu.PrefetchScalarGridSpec(
        num_scalar_prefetch=1,
        grid=(Bs,),
        in_specs=[pl.BlockSpec((1, H2, NK), lambda b, pt: (b, 0, 0)),
                  pl.BlockSpec((1, H2, LANES), lambda b, pt: (b, 0, 0)),
                  pl.BlockSpec((1, LAT_W, PAGE), lambda b, pt: (b, 0, 0)),
                  pl.BlockSpec(memory_space=pl.ANY),
                  pl.BlockSpec(wukt.shape, lambda b, pt: (0, 0)),
                  pl.BlockSpec(wuv.shape, lambda b, pt: (0, 0))],
        out_specs=pl.BlockSpec((1, 1, VW), lambda b, pt: (b, 0, 0)),
        scratch_shapes=[pltpu.VMEM((2, n_chunks, LAT_W, chunk), F32),
                        pltpu.SemaphoreType.DMA((2,)),
                        pltpu.VMEM((NK + H2, KV_LORA), BF16),
                        pltpu.VMEM((LANES, chunk), F32),
                        pltpu.VMEM((n_chunks, MLA_HEADS, chunk), F32),
                        pltpu.VMEM((n_chunks, KV_LORA, chunk), BF16)])
    out = pl.pallas_call(
        functools.partial(_attn_sample_kernel, layer=layer, n_pages=n_pages, chunk=chunk),
        grid_spec=grid_spec,
        out_shape=jax.ShapeDtypeStruct((Bs, 1, VW), F32),
        compiler_params=_cparams(1),
        name="attn_sample",
    )(page_table, qn, qr, new_t, cache_t, wukt, wuv)
    return out.reshape(Bs, VW)


def _mlstm_prompt_kernel(u_ref, v_ref, og_ref, gt_ref, cw_ref, cb_ref, wq_ref, wk_ref, gb_ref, ng_ref,
                         hm_ref, c_out, n_out, m_out,
                         uprev_ref, cs_ref, ns_ref, ms_ref, *, t_valid):
    L = SEQ_BLOCK
    i = pl.program_id(0)

    @pl.when(i == 0)
    def _():
        uprev_ref[...] = jnp.zeros_like(uprev_ref)
        cs_ref[...] = jnp.zeros_like(cs_ref)
        ns_ref[...] = jnp.zeros_like(ns_ref)
        ms_ref[...] = jnp.zeros_like(ms_ref)

    cw = cw_ref[...]
    row = lax.broadcasted_iota(jnp.int32, (L, 1), 0) + i * L
    valid = row < t_valid
    rid = lax.broadcasted_iota(jnp.int32, (L, L), 0)
    cid = lax.broadcasted_iota(jnp.int32, (L, L), 1)
    causal = cid <= rid
    tri = causal.astype(BF16)

    for bi, h in [(bi, h) for bi in range(u_ref.shape[0]) for h in range(ML_HEADS)]:
        if h == 0:
            u = u_ref[bi]
            prev8 = uprev_ref[bi]
            uc = cb_ref[...] + cw[3:4] * u
            for k in range(1, CONV_W):
                uc = uc + cw[3 - k:4 - k] * _shift_rows(u, prev8, k)
            uprev_ref[bi] = u[L - SUBLANES:, :]
            uc = uc * _sigmoid(uc)
            gb = gt_ref[bi] + gb_ref[...]
            li_all = jnp.where(valid, gb, NEG)
            lf_all = jnp.where(valid, _log_sigmoid(gb), 0.0)
            lf_hi, lf_lo = _split_bf16(lf_all)
            b_all = _dot(tri, lf_hi) + _dot(tri, lf_lo)
            li_t = li_all.T
            b_t = b_all.T
        sl = slice(h * ML_HEAD, (h + 1) * ML_HEAD)
        uh = uc[:, sl].astype(BF16)
        qh = _dot(uh, wq_ref[h])
        kh = _dot(uh, wk_ref[h]) * (ML_HEAD ** -0.5)
        vh = v_ref[bi, :, sl]
        qb, kb = qh.astype(BF16), kh.astype(BF16)
        li_col = li_all[:, h:h + 1]
        li_row = li_t[h:h + 1, :]
        b_col = b_all[:, ML_HEADS + h:ML_HEADS + h + 1]
        b_row = b_t[ML_HEADS + h:ML_HEADS + h + 1, :]
        m_prev = ms_ref[bi, h:h + 1, 0:1]
        n_row = ns_ref[bi, h:h + 1, :]
        c_prev = cs_ref[bi, h]

        dm = jnp.where(causal, b_col - b_row + li_row, NEG)
        inter = b_col + m_prev
        m_t = jnp.maximum(inter, jnp.max(dm, axis=-1, keepdims=True))
        w_intra = jnp.exp(dm - m_t)
        w_inter = jnp.exp(inter - m_t)
        s = _dot_nt(qb, kb) * w_intra
        num = _dot(s.astype(BF16), vh.astype(BF16)) + w_inter * _dot_nt(qb, c_prev.astype(BF16))
        den = jnp.sum(s, axis=-1, keepdims=True) + w_inter * jnp.sum(qh * n_row, axis=-1, keepdims=True)
        hh = num / jnp.maximum(jnp.abs(den), jnp.exp(-m_t))

        b_end = b_col[L - 1:L, :]
        g_col = b_end - b_col + li_col
        g_row = b_end - b_row + li_row
        m_new = jnp.maximum(b_end + m_prev, jnp.max(g_row, axis=-1, keepdims=True))
        wg_col = jnp.exp(g_col - m_new)
        wc = jnp.exp(b_end + m_prev - m_new)
        cs_ref[bi, h] = wc * c_prev + _dot_tn((vh * wg_col).astype(BF16), kb)
        ns_ref[bi, h:h + 1, :] = wc * n_row + jnp.sum(kh * wg_col, axis=0, keepdims=True)
        ms_ref[bi, h:h + 1, :] = jnp.broadcast_to(m_new, (1, LANES))

        hn = hh * lax.rsqrt(jnp.mean(hh * hh, axis=-1, keepdims=True) + EPS) * ng_ref[h:h + 1, :]
        hm_ref[bi, :, sl] = (hn * _sigmoid(og_ref[bi, :, sl])).astype(BF16)

    @pl.when(i == pl.num_programs(0) - 1)
    def _():
        c_out[...] = cs_ref[...]
        n_out[...] = ns_ref[:, :ML_HEADS, :]
        m_out[...] = ms_ref[:, :ML_HEADS, :]


def mlstm_prompt(z, cw, cb, wq, wk, gate_bias, ng, B, TP, t_valid):
    nc = TP // SEQ_BLOCK
    z3 = z.reshape(B, TP, z.shape[1])
    zcol = lambda off, w: pl.BlockSpec((B, SEQ_BLOCK, w), lambda i: (0, i, off // w))
    full = lambda a: pl.BlockSpec(a.shape, lambda i: (0,) * a.ndim)
    hm, c_p, n_p, m_p = pl.pallas_call(
        functools.partial(_mlstm_prompt_kernel, t_valid=t_valid),
        grid=(nc,),
        in_specs=[zcol(Z_U, ML_W), zcol(Z_V, ML_W), zcol(Z_OG, ML_W), zcol(Z_GT, LANES),
                  full(cw), full(cb), full(wq), full(wk), full(gate_bias), full(ng)],
        out_specs=[pl.BlockSpec((B, SEQ_BLOCK, ML_W), lambda i: (0, i, 0)),
                   pl.BlockSpec((B, ML_HEADS, ML_HEAD, ML_HEAD), lambda i: (0, 0, 0, 0)),
                   pl.BlockSpec((B, ML_HEADS, ML_HEAD), lambda i: (0, 0, 0)),
                   pl.BlockSpec((B, ML_HEADS, LANES), lambda i: (0, 0, 0))],
        out_shape=[jax.ShapeDtypeStruct((B, TP, ML_W), BF16),
                   jax.ShapeDtypeStruct((B, ML_HEADS, ML_HEAD, ML_HEAD), F32),
                   jax.ShapeDtypeStruct((B, ML_HEADS, ML_HEAD), F32),
                   jax.ShapeDtypeStruct((B, ML_HEADS, LANES), F32)],
        scratch_shapes=[pltpu.VMEM((B, SUBLANES, ML_W), F32),
                        pltpu.VMEM((B, ML_HEADS, ML_HEAD, ML_HEAD), F32),
                        pltpu.VMEM((B, SUBLANES, ML_HEAD), F32),
                        pltpu.VMEM((B, SUBLANES, LANES), F32)],
        compiler_params=_cparams(1),
        name="mlstm_prompt",
    )(z3, z3, z3, z3, cw, cb, wq, wk, gate_bias, ng)
    return hm.reshape(B * TP, ML_W), c_p, n_p, m_p


def _mlstm_sample_kernel(u_ref, v_ref, og_ref, gt_ref, buf_ref, c_ref, n_ref, m_ref,
                         cw_ref, cb_ref, wq_ref, wk_ref, gb_ref, ng_ref,
                         hm_ref, c_out, n_out, m_out, buf_out):
    u = u_ref[0]
    buf = buf_ref[0]
    cw = cw_ref[...]
    uc = cb_ref[...] + cw[3:4] * u
    for j in range(CONV_W - 1):
        uc = uc + cw[j:j + 1] * buf[j:j + 1]
    buf_out[0, 0:CONV_W - 2, :] = buf[1:CONV_W - 1]
    buf_out[0, CONV_W - 2:CONV_W - 1, :] = u
    uc = uc * _sigmoid(uc)
    gb = gt_ref[0] + gb_ref[...]
    lane = lax.broadcasted_iota(jnp.int32, (1, LANES), 1)
    m_all = m_ref[0]
    m_acc = jnp.zeros((1, LANES), F32)
    for h in range(ML_HEADS):
        sl = slice(h * ML_HEAD, (h + 1) * ML_HEAD)
        uh = jnp.broadcast_to(uc[:, sl], (ROWS16, ML_HEAD))
        qh = _dot3(uh, wq_ref[h])
        kh = _dot3(uh, wk_ref[h]) * (ML_HEAD ** -0.5)
        vh = v_ref[0][:, sl]
        li = gb[:, h:h + 1]
        lf = _log_sigmoid(gb[:, ML_HEADS + h:ML_HEADS + h + 1])
        m_prev = m_all[:, h:h + 1]
        c_prev = c_ref[0, h]
        n_row = n_ref[0, h:h + 1, :]
        m_t = jnp.maximum(lf + m_prev, li)
        w_in = jnp.exp(li - m_t)
        w_st = jnp.exp(lf + m_prev - m_t)
        s = jnp.sum(qh[0:1] * kh[0:1], axis=-1, keepdims=True) * w_in
        cq = _dot3(qh, c_prev, _dot_nt)[0:1]
        num = s * vh + w_st * cq
        den = s + w_st * jnp.sum(qh[0:1] * n_row, axis=-1, keepdims=True)
        hh = num / jnp.maximum(jnp.abs(den), jnp.exp(-m_t))
        v16 = jnp.where(lax.broadcasted_iota(jnp.int32, (ROWS16, ML_HEAD), 0) == 0,
                        jnp.broadcast_to(vh * w_in, (ROWS16, ML_HEAD)), 0.0)
        c_out[0, h] = w_st * c_prev + _dot3(v16, kh, _dot_tn)
        n_out[0, h:h + 1, :] = w_st * n_row + w_in * kh[0:1]
        m_acc = m_acc + jnp.where(lane == h, m_t, 0.0)
        hn = hh * lax.rsqrt(jnp.mean(hh * hh, axis=-1, keepdims=True) + EPS) * ng_ref[h:h + 1, :]
        hm_ref[0, :, sl] = hn * _sigmoid(og_ref[0][:, sl])
    m_out[0] = m_acc


def mlstm_sample(zs, buf, C, n, m, cw, cb, wq, wk, gate_bias, ng):
    Bs = zs.shape[0]
    mp = jnp.pad(m, ((0, 0), (0, LANES - ML_HEADS))).reshape(Bs, 1, LANES)
    zcol = lambda off, w: pl.BlockSpec((1, 1, w), lambda b: (b, 0, off // w))
    full = lambda a: pl.BlockSpec(a.shape, lambda b: (0,) * a.ndim)
    seq = lambda a: pl.BlockSpec((1,) + a.shape[1:], lambda b: (b,) + (0,) * (a.ndim - 1))
    outs = pl.pallas_call(
        _mlstm_sample_kernel,
        grid=(Bs,),
        in_specs=[zcol(Z_U, ML_W), zcol(Z_V, ML_W), zcol(Z_OG, ML_W), zcol(Z_GT, LANES),
                  seq(buf), seq(C), seq(n), seq(mp),
                  full(cw), full(cb), full(wq), full(wk), full(gate_bias), full(ng)],
        out_specs=[pl.BlockSpec((1, 1, ML_W), lambda b: (b, 0, 0)),
                   seq(C), seq(n), seq(mp), seq(buf)],
        out_shape=[jax.ShapeDtypeStruct((Bs, 1, ML_W), F32),
                   jax.ShapeDtypeStruct(C.shape, F32),
                   jax.ShapeDtypeStruct(n.shape, F32),
                   jax.ShapeDtypeStruct(mp.shape, F32),
                   jax.ShapeDtypeStruct(buf.shape, F32)],
        compiler_params=_cparams(1),
        name="mlstm_sample",
    )(zs, zs, zs, zs, buf, C, n, mp, cw, cb, wq, wk, gate_bias, ng)
    hm, C_new, n_new, m_new, buf_new = outs
    return hm.reshape(Bs, ML_W), C_new, n_new, m_new.reshape(Bs, LANES)[:, :ML_HEADS], buf_new


def _lru_gates(xc, wa_ref, wx_ref, ba, bx, c, precise=False):
    rs, iss = [], []
    for g in range(LRU_BLOCKS):
        xg = xc[:, g * LANES:(g + 1) * LANES]
        rs.append(_mm(xg, wa_ref[g], precise))
        iss.append(_mm(xg, wx_ref[g], precise))
    r = _sigmoid(jnp.concatenate(rs, axis=-1) + ba)
    ig = _sigmoid(jnp.concatenate(iss, axis=-1) + bx)
    a = jnp.exp(c * r)
    return a, jnp.sqrt(1.0 - a * a) * ig * xc


def _rglru_prompt_kernel(gate_ref, xb_ref, cw_ref, cb_ref, wa_ref, wx_ref, ba_ref, bx_ref, c_ref,
                         y_ref, h_out, xprev_ref, hs_ref, *, t_valid):
    T = SEQ_BLOCK
    i = pl.program_id(1)

    @pl.when(i == 0)
    def _():
        xprev_ref[...] = jnp.zeros_like(xprev_ref)
        hs_ref[...] = jnp.zeros_like(hs_ref)

    xb = xb_ref[...]
    prev8 = xprev_ref[...]
    cw = cw_ref[...]
    xc = cb_ref[...] + cw[3:4] * xb
    for k in range(1, CONV_W):
        xc = xc + cw[3 - k:4 - k] * _shift_rows(xb, prev8, k)
    xprev_ref[...] = xb[T - SUBLANES:, :]

    a, bx = _lru_gates(xc, wa_ref, wx_ref, ba_ref[...], bx_ref[...], c_ref[...])
    row = lax.broadcasted_iota(jnp.int32, (T, 1), 0)
    valid = row + i * T < t_valid
    a = jnp.where(valid, a, 1.0)
    bx = jnp.where(valid, bx, 0.0)
    k = 1
    while k < T:
        keep = row >= k
        a_sh = jnp.where(keep, pltpu.roll(a, k, 0), 1.0)
        b_sh = jnp.where(keep, pltpu.roll(bx, k, 0), 0.0)
        bx = a * b_sh + bx
        a = a * a_sh
        k *= 2
    hs = a * hs_ref[0:1, :] + bx
    hs_ref[...] = jnp.broadcast_to(hs[T - 1:T, :], hs_ref.shape)
    y_ref[...] = (hs * jax.nn.gelu(gate_ref[...])).astype(BF16)

    @pl.when(i == pl.num_programs(1) - 1)
    def _():
        h_out[0] = hs[T - 1:T, :]


def rglru_prompt(zz, cw, cb, wa, wx, ba, bx, c, B, TP, t_valid):
    W = zz.shape[1] // 2
    nb = TP // SEQ_BLOCK
    full = lambda a: pl.BlockSpec(a.shape, lambda b, i: (0,) * a.ndim)
    return pl.pallas_call(
        functools.partial(_rglru_prompt_kernel, t_valid=t_valid),
        grid=(B, nb),
        in_specs=[pl.BlockSpec((SEQ_BLOCK, W), lambda b, i: (b * nb + i, 0)),
                  pl.BlockSpec((SEQ_BLOCK, W), lambda b, i: (b * nb + i, 1)),
                  full(cw), full(cb), full(wa), full(wx), full(ba), full(bx), full(c)],
        out_specs=[pl.BlockSpec((SEQ_BLOCK, W), lambda b, i: (b * nb + i, 0)),
                   pl.BlockSpec((1, 1, W), lambda b, i: (b, 0, 0))],
        out_shape=[jax.ShapeDtypeStruct((B * TP, W), BF16),
                   jax.ShapeDtypeStruct((B, 1, W), F32)],
        scratch_shapes=[pltpu.VMEM((SUBLANES, W), F32), pltpu.VMEM((SUBLANES, W), F32)],
        compiler_params=_cparams(2),
        name="rglru_prompt",
    )(zz, zz, cw, cb, wa, wx, ba, bx, c)


def _rglru_sample_kernel(gate_ref, xb_ref, b0_ref, b1_ref, b2_ref, h0_ref,
                         cw_ref, cb_ref, wa_ref, wx_ref, ba_ref, bx_ref, c_ref,
                         y_ref, h_out):
    xb = xb_ref[...]
    cw = cw_ref[...]
    xc = cb_ref[...] + cw[0:1] * b0_ref[...] + cw[1:2] * b1_ref[...] + cw[2:3] * b2_ref[...] + cw[3:4] * xb
    a, bx = _lru_gates(xc, wa_ref, wx_ref, ba_ref[...], bx_ref[...], c_ref[...], precise=True)
    hs = a * h0_ref[...] + bx
    h_out[...] = hs
    y_ref[...] = hs * jax.nn.gelu(gate_ref[...])


def rglru_sample(gate, xb, buf, h0, cw, cb, wa, wx, ba, bx, c):
    Bs, W = xb.shape
    args = (gate, xb, buf[:, 0], buf[:, 1], buf[:, 2], h0, cw, cb, wa, wx, ba, bx, c)
    full = lambda a: pl.BlockSpec(a.shape, lambda i: (0,) * a.ndim)
    return pl.pallas_call(
        _rglru_sample_kernel,
        grid=(1,),
        in_specs=[full(a) for a in args],
        out_specs=[pl.BlockSpec((Bs, W), lambda i: (0, 0)), pl.BlockSpec((Bs, W), lambda i: (0, 0))],
        out_shape=[jax.ShapeDtypeStruct((Bs, W), F32), jax.ShapeDtypeStruct((Bs, W), F32)],
        compiler_params=_cparams(1),
        name="rglru_sample",
    )(*args)


def _pad_cols(w, n):
    return jnp.pad(w, ((0, 0), (0, n - w.shape[1])))


def _even_in_proj(w):
    o = 0
    pieces = {}
    for name, width in (("cq", Q_LORA), ("ckv", KV_LORA), ("kr", QK_ROPE), ("u", ML_W), ("v", ML_W),
                        ("og", ML_W), ("ig", ML_HEADS), ("fg", ML_HEADS)):
        pieces[name] = w[:, o:o + width]
        o += width
    gates = _pad_cols(jnp.concatenate([pieces["ig"], pieces["fg"]], axis=1), Z_W - Z_GT)
    return jnp.concatenate([pieces["cq"], _pad_cols(pieces["kr"], LANES), pieces["u"], pieces["v"],
                            pieces["og"], pieces["ckv"], gates], axis=1)


def _head_pad(w):
    K = w.shape[0]
    d = w.shape[1] // MLA_HEADS
    w3 = w.reshape(K, MLA_HEADS, d)
    return jnp.pad(w3, ((0, 0), (0, 0), (0, HEAD_PAD - d))).reshape(K, MLA_HEADS * HEAD_PAD)


def _rope_tables(pos):
    half = QK_ROPE // 2
    inv_freq = ROPE_THETA ** (-jnp.arange(half, dtype=F32) / half)
    ang = pos.astype(F32)[:, None] * inv_freq[None, :]
    cos, sin = jnp.cos(ang), jnp.sin(ang)
    n = pos.shape[0]
    one = jnp.ones((n, QK_NOPE), F32)
    zh = jnp.zeros((n, half), F32)
    zn = jnp.zeros((n, QK_NOPE), F32)
    zt = jnp.zeros((n, HEAD_PAD - QK_HEAD), F32)
    tc = jnp.concatenate([one, cos, cos, zt], axis=1)
    ts1 = jnp.concatenate([zn, zh, sin, zt], axis=1)
    ts2 = jnp.concatenate([zn, -sin, zh, zt], axis=1)
    return tc, ts1, ts2


def kernel(x_prompt, x_sample, cache_mla, state_mlstm_C, state_mlstm_n, state_mlstm_m, state_mlstm_conv, state_rglru_h, state_rglru_conv, page_table, meta_tokens, norm_mix_even, w_in_even, mla_q_a_g, mla_w_uq, mla_kv_a_g, mla_w_uk, mla_w_uv, mla_q_norm_g, mla_k_norm_g, ml_conv_w, ml_conv_b, ml_w_q, ml_w_k, ml_b_i, ml_b_f, ml_norm_g, w_out_even, norm_ffn_even, ffn_w1, ffn_w3, ffn_w2, norm_mix_odd, w_in_odd, lru_conv_w, lru_conv_b, lru_w_a, lru_b_a, lru_w_x, lru_b_x, lru_lam, w_out_odd, norm_ffn_odd, moe_router, moe_w1, moe_w3, moe_w2):
    B, S, D = x_prompt.shape
    Bs = x_sample.shape[0]
    assert x_sample.shape[1] == 1
    Tp = N_META + S
    TP = _round_up(Tp, SEQ_BLOCK)
    n_prompt = B * TP
    past_len = page_table.shape[1] * PAGE
    depth = norm_mix_even.shape[0] + norm_mix_odd.shape[0]

    meta = jnp.broadcast_to(meta_tokens[None], (B, N_META, D))
    xp = jnp.pad(jnp.concatenate([meta, x_prompt], axis=1), ((0, 0), (0, TP - Tp), (0, 0)))
    x = xp.reshape(n_prompt, D)
    xs = x_sample.reshape(Bs, D)
    tabs = _rope_tables(jnp.tile(jnp.arange(TP, dtype=jnp.int32), B))
    tabs_s = _rope_tables(jnp.full((Bs,), past_len, jnp.int32))
    bf = lambda a: a.astype(BF16)
    cache_t = jnp.swapaxes(cache_mla, 2, 3)
    ffn_w = tuple(w[:, None] for w in (ffn_w1, ffn_w3, ffn_w2))
    ffn_bf = tuple(bf(w) for w in ffn_w)
    moe_bf = tuple(bf(w) for w in (moe_w1, moe_w3, moe_w2))

    def prompt_view(a):
        return a.reshape((B, TP) + a.shape[1:])

    rows_p, rows_s, Cp, Cs, np_, ns, mp, ms, cbp, cbs = [], [], [], [], [], [], [], [], [], []
    hp_l, hs_l, lbp, lbs = [], [], [], []
    for layer in range(depth):
        if layer % 2 == 0:
            e = layer // 2
            w_in = _even_in_proj(w_in_even[e])
            z = norm_matmul(x, norm_mix_even[e], bf(w_in))
            zs = norm_matmul(xs, norm_mix_even[e], w_in, precise=True)
            qng = jnp.pad(mla_q_norm_g[e] * (QK_HEAD ** -0.5), (0, HEAD_PAD - QK_HEAD)).reshape(1, HEAD_PAD)
            kng = jnp.pad(mla_k_norm_g[e], (0, HEAD_PAD - QK_HEAD)).reshape(1, HEAD_PAD)
            wuk2 = mla_w_uk[e].reshape(KV_LORA, MLA_HEADS * QK_NOPE)
            wuv2 = mla_w_uv[e].reshape(KV_LORA, MLA_HEADS * V_HEAD)
            prep_g = (mla_q_a_g[e].reshape(1, Q_LORA), mla_kv_a_g[e].reshape(1, KV_LORA), qng, kng)
            prep_w = (_head_pad(mla_w_uq[e]), _head_pad(wuk2), wuv2)
            q, k, v, rows = mla_prep(z, tabs, *prep_g, *(bf(w) for w in prep_w))
            q_s, _, _, rows_sample = mla_prep(zs, tabs_s, *prep_g, *prep_w, precise=True)
            att_p = attn_prompt(q, k, v, B, TP)
            qs = q_s.reshape(Bs, MLA_HEADS, HEAD_PAD) * kng.reshape(1, 1, HEAD_PAD)
            eye = jnp.eye(MLA_HEADS, dtype=F32)
            qn_bd = (qs[:, :, None, :QK_NOPE] * eye[None, :, :, None]).reshape(Bs, MLA_HEADS, MLA_HEADS * QK_NOPE)
            qr_p = jnp.pad(qs[:, :, QK_NOPE:QK_HEAD], ((0, 0), (0, 0), (0, LANES - QK_ROPE)))
            pad_heads = lambda a: bf(jnp.pad(a, ((0, 0), (0, MLA_HEADS), (0, 0))))
            att_s = attn_sample(pad_heads(qn_bd), pad_heads(qr_p), rows_sample, cache_t, e, page_table,
                                bf(wuk2.T), bf(wuv2))
            gate_bias = jnp.pad(jnp.concatenate([ml_b_i[e], ml_b_f[e]]), (0, LANES - 2 * ML_HEADS)).reshape(1, LANES)
            ml_conv = (ml_conv_w[e], ml_conv_b[e].reshape(1, ML_W))
            hm_p, c_p, n_p, m_p = mlstm_prompt(z, *ml_conv, bf(ml_w_q[e]), bf(ml_w_k[e]), gate_bias, ml_norm_g[e],
                                               B, TP, Tp)
            hm_s, c_s, n_s, m_s, cb_s = mlstm_sample(zs.reshape(Bs, 1, Z_W), state_mlstm_conv[e], state_mlstm_C[e],
                                                     state_mlstm_n[e], state_mlstm_m[e], *ml_conv,
                                                     ml_w_q[e], ml_w_k[e], gate_bias, ml_norm_g[e])
            x = matmul_residual(jnp.concatenate([att_p, hm_p], axis=1), bf(w_out_even[e]), x)
            xs = matmul_residual(jnp.concatenate([att_s, hm_s], axis=1), w_out_even[e], xs, precise=True)
            x = swiglu_residual(x, norm_ffn_even[e], *ffn_bf, e)
            xs = swiglu_residual(xs, norm_ffn_even[e], *ffn_w, e, precise=True)
            rows_p.append(prompt_view(rows)[:, :Tp])
            rows_s.append(rows_sample.reshape(Bs, 1, LAT_W))
            Cp.append(c_p); np_.append(n_p); mp.append(m_p[:, :, 0])
            Cs.append(c_s); ns.append(n_s); ms.append(m_s)
            cbp.append(prompt_view(z)[:, Tp - (CONV_W - 1):Tp, Z_U:Z_U + ML_W])
            cbs.append(cb_s)
        else:
            o = layer // 2
            W = lru_conv_w.shape[2]
            zz = norm_matmul(x, norm_mix_odd[o], bf(w_in_odd[o]))
            zs = norm_matmul(xs, norm_mix_odd[o], w_in_odd[o], precise=True)
            c = (-LRU_C * jax.nn.softplus(-lru_lam[o])).reshape(1, W)
            lru_conv = (lru_conv_w[o], lru_conv_b[o].reshape(1, W))
            lru_bias = (lru_b_a[o].reshape(1, W), lru_b_x[o].reshape(1, W), c)
            y_p, h_p = rglru_prompt(zz, *lru_conv, bf(lru_w_a[o]), bf(lru_w_x[o]), *lru_bias, B, TP, Tp)
            y_s, h_s = rglru_sample(zs[:, :W], zs[:, W:], state_rglru_conv[o], state_rglru_h[o], *lru_conv,
                                    lru_w_a[o], lru_w_x[o], *lru_bias)
            x, x2d = matmul_residual(y_p, bf(w_out_odd[o]), x, row_tiles=True)
            xs = matmul_residual(y_s, w_out_odd[o], xs, precise=True)
            x = moe_sparse_residual(x, x2d, norm_ffn_odd[o], router(x, norm_ffn_odd[o], moe_router[o]),
                                    *moe_bf, o)
            xs = swiglu_residual(xs, norm_ffn_odd[o], *moe_bf, o, router(xs, norm_ffn_odd[o], moe_router[o]))
            hp_l.append(h_p[:, 0]); hs_l.append(h_s)
            lbp.append(prompt_view(zz)[:, Tp - (CONV_W - 1):Tp, W:])
            lbs.append(jnp.concatenate([state_rglru_conv[o][:, 1:], zs[:, None, W:]], axis=1))
    y_prompt = prompt_view(x)[:, N_META:Tp]
    y_sample = xs.reshape(Bs, 1, D)
    return (y_prompt, y_sample,
            jnp.stack(rows_p), jnp.stack(rows_s),
            jnp.stack(Cp), jnp.stack(Cs), jnp.stack(np_), jnp.stack(ns), jnp.stack(mp), jnp.stack(ms),
            jnp.stack(cbp), jnp.stack(cbs),
            jnp.stack(hp_l), jnp.stack(hs_l), jnp.stack(lbp), jnp.stack(lbs))
```

```python
import functools

import jax
import jax.numpy as jnp
from jax import lax
from jax.experimental import pallas as pl
from jax.experimental.pallas import tpu as pltpu

F32 = jnp.float32
BF16 = jnp.bfloat16

N_META = 16
EPS = 1e-6
MLA_HEADS = 8
Q_LORA = 384
KV_LORA = 256
QK_NOPE = 64
QK_ROPE = 32
V_HEAD = 64
QK_HEAD = QK_NOPE + QK_ROPE
LAT_W = KV_LORA + QK_ROPE
ROPE_THETA = 10000.0
ML_HEADS = 4
ML_HEAD = 128
ML_W = ML_HEADS * ML_HEAD
CONV_W = 4
LRU_BLOCKS = 8
LRU_C = 8.0
TOP_K = 2
PAGE = 128

LANES = 128
SUBLANES = 8
ROWS16 = 16
HEAD_PAD = 128
SEQ_BLOCK = 128
NEG = -1e30
VMEM_LIMIT = 56 * 1024 * 1024

R_IDX, R_GATE = 8, 10
ROW_TILE = SUBLANES * LANES

Z_CQ, Z_KR, Z_U, Z_V, Z_OG, Z_CKV, Z_GT, Z_W = 0, 384, 512, 1024, 1536, 2048, 2304, 2560


def _cparams(n_axes):
    return pltpu.CompilerParams(dimension_semantics=("arbitrary",) * n_axes,
                                vmem_limit_bytes=VMEM_LIMIT)


def _round_up(a, b):
    return (a + b - 1) // b * b


def _pick_tile(n, prefs):
    for t in prefs:
        if n % t == 0:
            return t
    if n < min(prefs):
        return n
    raise ValueError(f"no tile in {prefs} divides {n}")


def _sigmoid(x):
    return 1.0 / (1.0 + jnp.exp(-x))


def _log_sigmoid(x):
    return jnp.minimum(x, 0.0) - jnp.log(1.0 + jnp.exp(-jnp.abs(x)))


def _dot(a, b):
    return jnp.dot(a, b, preferred_element_type=F32)


def _dot_nt(a, b):
    return lax.dot_general(a, b, (((1,), (1,)), ((), ())), preferred_element_type=F32)


def _dot_tn(a, b):
    return lax.dot_general(a, b, (((0,), (0,)), ((), ())), preferred_element_type=F32)


def _split_bf16(x):
    hi = x.astype(BF16)
    lo = (x - hi.astype(F32)).astype(BF16)
    return hi, lo


def _dot3(a, b, dot=_dot):
    ah, al = _split_bf16(a)
    bh, bl = _split_bf16(b)
    return dot(ah, bh) + dot(al, bh) + dot(ah, bl)


def _mm(a, w, precise):
    if precise:
        return _dot3(a.astype(F32), w)
    return _dot(a.astype(BF16), w)


def _shift_rows(x, prev8, k):
    rolled = pltpu.roll(x, k, 0)
    head = pltpu.roll(prev8, k, 0)
    rid = lax.broadcasted_iota(jnp.int32, (SUBLANES, x.shape[1]), 0)
    top = jnp.where(rid < k, head, rolled[:SUBLANES])
    if x.shape[0] == SUBLANES:
        return top
    return jnp.concatenate([top, rolled[SUBLANES:]], axis=0)


def _norm_mm_kernel(x_ref, g_ref, w_ref, o_ref, *, precise):
    x = x_ref[...]
    ms = jnp.mean(x * x, axis=-1, keepdims=True)
    xn = x * lax.rsqrt(ms + EPS) * g_ref[...]
    o_ref[...] = _mm(xn, w_ref[...], precise)


def norm_matmul(x, g, w, precise=False):
    M, K = x.shape
    N = w.shape[1]
    tm = _pick_tile(M, (512, 256, 128))
    return pl.pallas_call(
        functools.partial(_norm_mm_kernel, precise=precise),
        grid=(M // tm,),
        in_specs=[pl.BlockSpec((tm, K), lambda i: (i, 0)),
                  pl.BlockSpec((1, K), lambda i: (0, 0)),
                  pl.BlockSpec((K, N), lambda i: (0, 0))],
        out_specs=pl.BlockSpec((tm, N), lambda i: (i, 0)),
        out_shape=jax.ShapeDtypeStruct((M, N), F32),
        compiler_params=_cparams(1),
        name="norm_matmul",
    )(x, g.reshape(1, K), w)


def _mm_res_kernel(a_ref, w_ref, r_ref, o_ref, *tile_ref, precise):
    o = r_ref[...] + _mm(a_ref[...], w_ref[...], precise)
    o_ref[...] = o
    if tile_ref:
        _store_row_tiles(tile_ref[0], o)


def _store_row_tiles(ref, rows):
    n = rows.shape[0]
    for k in range(SUBLANES):
        ref[pl.ds(k, n, stride=SUBLANES), :] = rows[:, k * LANES:(k + 1) * LANES]


def _load_row_tiles(ref, n):
    return jnp.concatenate([ref[pl.ds(k, n, stride=SUBLANES), :] for k in range(SUBLANES)], axis=1)


def matmul_residual(a, w, res, precise=False, row_tiles=False):
    M, K = a.shape
    N = w.shape[1]
    tm = _pick_tile(M, (1024, 768, 512, 256, 128))
    out_specs = [pl.BlockSpec((tm, N), lambda i: (i, 0))]
    out_shape = [jax.ShapeDtypeStruct((M, N), F32)]
    if row_tiles:
        assert N == ROW_TILE
        out_specs.append(pl.BlockSpec((tm * SUBLANES, LANES), lambda i: (i, 0)))
        out_shape.append(jax.ShapeDtypeStruct((M * SUBLANES, LANES), F32))
    out = pl.pallas_call(
        functools.partial(_mm_res_kernel, precise=precise),
        grid=(M // tm,),
        in_specs=[pl.BlockSpec((tm, K), lambda i: (i, 0)),
                  pl.BlockSpec((K, N), lambda i: (0, 0)),
                  pl.BlockSpec((tm, N), lambda i: (i, 0))],
        out_specs=out_specs,
        out_shape=out_shape,
        compiler_params=_cparams(1),
        name="matmul_residual",
    )(a, w, res)
    return out if row_tiles else out[0]


def _ffn_kernel(*refs, gated, precise):
    if gated:
        x_ref, g_ref, comb_ref, w1_ref, w3_ref, w2_ref, o_ref, xn_ref, acc_ref = refs
    else:
        x_ref, g_ref, w1_ref, w3_ref, w2_ref, o_ref, xn_ref, acc_ref = refs
    e = pl.program_id(1)
    f = pl.program_id(2)
    first = jnp.logical_and(e == 0, f == 0)
    last = jnp.logical_and(e == pl.num_programs(1) - 1, f == pl.num_programs(2) - 1)

    @pl.when(first)
    def _():
        x = x_ref[...]
        ms = jnp.mean(x * x, axis=-1, keepdims=True)
        xn_ref[...] = (x * lax.rsqrt(ms + EPS) * g_ref[...]).astype(xn_ref.dtype)
        acc_ref[...] = jnp.zeros_like(acc_ref)

    xn = xn_ref[...]
    h1 = _mm(xn, w1_ref[0, 0], precise)
    h3 = _mm(xn, w3_ref[0, 0], precise)
    a = h1 * _sigmoid(h1) * h3
    if gated:
        comb = comb_ref[...]
        lane = lax.broadcasted_iota(jnp.int32, comb.shape, 1)
        a = a * jnp.sum(jnp.where(lane == e, comb, 0.0), axis=-1, keepdims=True)
    acc_ref[...] += _mm(a, w2_ref[0, 0], precise)

    @pl.when(last)
    def _():
        o_ref[...] = x_ref[...] + acc_ref[...]


def swiglu_residual(x, g, w1, w3, w2, layer, comb=None, precise=False):
    M, D = x.shape
    _, E, _, F = w1.shape
    tm = _pick_tile(M, (1536, 1024, 768, 512, 256, 128))
    tf = _pick_tile(F, (256, 128))
    gated = comb is not None
    in_specs = [pl.BlockSpec((tm, D), lambda i, e, f: (i, 0)),
                pl.BlockSpec((1, D), lambda i, e, f: (0, 0))]
    args = [x, g.reshape(1, D)]
    if gated:
        in_specs.append(pl.BlockSpec((tm, LANES), lambda i, e, f: (i, 0)))
        args.append(comb)
    in_specs += [pl.BlockSpec((1, 1, D, tf), lambda i, e, f: (layer, e, 0, f)),
                 pl.BlockSpec((1, 1, D, tf), lambda i, e, f: (layer, e, 0, f)),
                 pl.BlockSpec((1, 1, tf, D), lambda i, e, f: (layer, e, f, 0))]
    args += [w1, w3, w2]
    return pl.pallas_call(
        functools.partial(_ffn_kernel, gated=gated, precise=precise),
        grid=(M // tm, E, F // tf),
        in_specs=in_specs,
        out_specs=pl.BlockSpec((tm, D), lambda i, e, f: (i, 0)),
        out_shape=jax.ShapeDtypeStruct((M, D), F32),
        scratch_shapes=[pltpu.VMEM((tm, D), F32 if precise else BF16), pltpu.VMEM((tm, D), F32)],
        compiler_params=_cparams(3),
        name="swiglu_gated" if gated else "swiglu",
    )(*args)


def _router_kernel(x_ref, g_ref, w_ref, o_ref, *, n_experts):
    x = x_ref[...]
    ms = jnp.mean(x * x, axis=-1, keepdims=True)
    logits = _dot3(x * lax.rsqrt(ms + EPS) * g_ref[...], w_ref[...])
    lane = lax.broadcasted_iota(jnp.int32, logits.shape, 1)
    lg = jnp.where(lane < n_experts, logits, -jnp.inf)
    v1 = jnp.max(lg, axis=-1, keepdims=True)
    lane_f = lane.astype(F32)
    i1 = jnp.min(jnp.where(lg == v1, lane_f, float(LANES)), axis=-1, keepdims=True)
    lg2 = jnp.where(lane_f == i1, -jnp.inf, lg)
    v2 = jnp.max(lg2, axis=-1, keepdims=True)
    i2 = jnp.min(jnp.where(lg2 == v2, lane_f, float(LANES)), axis=-1, keepdims=True)
    ex = jnp.exp(v2 - v1)
    g1 = 1.0 / (1.0 + ex)
    g2 = ex / (1.0 + ex)
    comb = jnp.where(lane_f == i1, g1, 0.0) + jnp.where(lane_f == i2, g2, 0.0)
    for col, val in ((R_IDX, i1), (R_IDX + 1, i2), (R_GATE, g1), (R_GATE + 1, g2)):
        comb = jnp.where(lane == col, val, comb)
    o_ref[...] = comb


def router(x, g, w_router):
    M, D = x.shape
    E = w_router.shape[1]
    assert E <= R_IDX
    tm = _pick_tile(M, (512, 256, 128))
    return pl.pallas_call(
        functools.partial(_router_kernel, n_experts=E),
        grid=(M // tm,),
        in_specs=[pl.BlockSpec((tm, D), lambda i: (i, 0)),
                  pl.BlockSpec((1, D), lambda i: (0, 0)),
                  pl.BlockSpec((D, LANES), lambda i: (0, 0))],
        out_specs=pl.BlockSpec((tm, LANES), lambda i: (i, 0)),
        out_shape=jax.ShapeDtypeStruct((M, LANES), F32),
        compiler_params=_cparams(1),
        name="router",
    )(x, g.reshape(1, D), jnp.pad(w_router, ((0, 0), (0, LANES - E))))


def _moe_group_kernel(te_ref, ns_ref, src_ref, x2d_ref, g_ref, gate_ref, w1_ref, w3_ref, w2_ref,
                      o_ref, buf_ref, sem_ref, xn_ref, acc_ref, *, tg, sub):
    t = pl.program_id(0)
    f = pl.program_id(1)
    nt = pl.num_programs(0)
    slot = t % 2

    def issue(tile, s):
        def body(r2, _):
            for j in range(2):
                r = 2 * r2 + j
                tok = src_ref[tile * tg + r]
                pltpu.make_async_copy(x2d_ref.at[pl.ds(pl.multiple_of(tok * SUBLANES, SUBLANES), SUBLANES), :],
                                      buf_ref.at[s, pl.ds(pl.multiple_of(r * SUBLANES, SUBLANES), SUBLANES), :],
                                      sem_ref.at[s]).start(priority=j)
            return 0
        lax.fori_loop(0, tg // 2, body, 0, unroll=2)

    @pl.when(f == 0)
    def _():
        @pl.when(t == 0)
        def _():
            issue(0, 0)

        pltpu.make_async_copy(x2d_ref.at[pl.ds(0, tg * SUBLANES), :], buf_ref.at[slot], sem_ref.at[slot]).wait()

        @pl.when(t + 1 < nt)
        def _():
            issue(t + 1, 1 - slot)

        x = _load_row_tiles(buf_ref.at[slot], tg)
        ms = jnp.mean(x * x, axis=-1, keepdims=True)
        xn_ref[...] = (x * lax.rsqrt(ms + EPS) * g_ref[...]).astype(BF16)
        acc_ref[...] = jnp.zeros_like(acc_ref)

    w1 = w1_ref[0, 0]
    w3 = w3_ref[0, 0]
    w2 = w2_ref[0, 0]
    n_sub = ns_ref[t]
    n_full = tg // sub

    def expert_rows(rows):
        xn = xn_ref[rows, :]
        h1 = _dot(xn, w1)
        h3 = _dot(xn, w3)
        acc_ref[rows, :] += _dot((h1 * _sigmoid(h1) * h3).astype(BF16), w2)

    @pl.when(n_sub == n_full)
    def _():
        expert_rows(slice(0, tg))

    for s in range(n_full - 1):
        @pl.when(jnp.logical_and(s < n_sub, n_sub < n_full))
        def _():
            expert_rows(slice(s * sub, (s + 1) * sub))

    @pl.when(f == pl.num_programs(1) - 1)
    def _():
        _store_row_tiles(o_ref, acc_ref[...] * gate_ref[...])


def _moe_combine_kernel(p0_ref, p1_ref, x_ref, y2d_ref, o_ref, bufa_ref, bufb_ref, sem_ref, *, tc):
    i = pl.program_id(0)

    def body(r, _):
        dst = pl.ds(pl.multiple_of(r * SUBLANES, SUBLANES), SUBLANES)
        for p_ref, buf, s in ((p0_ref, bufa_ref, 0), (p1_ref, bufb_ref, 1)):
            pos = p_ref[i * tc + r]
            pltpu.make_async_copy(y2d_ref.at[pl.ds(pl.multiple_of(pos * SUBLANES, SUBLANES), SUBLANES), :],
                                  buf.at[dst, :], sem_ref.at[s]).start(priority=s)
        return 0

    lax.fori_loop(0, tc, body, 0)
    for buf, s in ((bufa_ref, 0), (bufb_ref, 1)):
        pltpu.make_async_copy(y2d_ref.at[pl.ds(0, tc * SUBLANES), :], buf, sem_ref.at[s]).wait()
    bufa_ref[...] = bufa_ref[...] + bufb_ref[...]
    o_ref[...] = x_ref[...] + _load_row_tiles(bufa_ref, tc)


def moe_sparse_residual(x, x2d, g, comb, w1, w3, w2, layer):
    M, D = x.shape
    assert D == ROW_TILE
    _, E, _, F = w1.shape
    S = TOP_K * M
    tg = 1024 if S >= 8192 else 256
    sub = tg // 4
    tf = _pick_tile(F, (512, 256, 128))
    n_tiles = S // tg + E
    P = n_tiles * tg

    e_flat = comb[:, R_IDX:R_IDX + TOP_K].astype(jnp.int32).reshape(S)
    gates = comb[:, R_GATE:R_GATE + TOP_K].reshape(S)
    onehot = (e_flat[:, None] == jnp.arange(E, dtype=jnp.int32)[None, :]).astype(jnp.int32)
    csum = jnp.cumsum(onehot, axis=0)
    counts = csum[-1]
    rank = jnp.take_along_axis(csum, e_flat[:, None], axis=1)[:, 0] - 1
    padded = (counts + tg - 1) // tg * tg
    ends = jnp.cumsum(padded)
    starts = ends - padded
    pos = starts[e_flat] + rank
    slot_of = jnp.full((P,), S, jnp.int32).at[pos].set(jnp.arange(S, dtype=jnp.int32))
    src = jnp.where(slot_of < S, slot_of // TOP_K, 0)
    gate_sorted = jnp.concatenate([gates, jnp.zeros((1,), F32)])[slot_of].reshape(P, 1)
    tile_start = jnp.arange(n_tiles, dtype=jnp.int32) * tg
    tile_e = jnp.minimum(jnp.sum((tile_start[:, None] >= ends[None, :]).astype(jnp.int32), axis=1), E - 1)
    valid = jnp.clip(starts[tile_e] + counts[tile_e] - tile_start, 0, tg)
    n_sub = (valid + sub - 1) // sub

    grid_spec = pltpu.PrefetchScalarGridSpec(
        num_scalar_prefetch=3,
        grid=(n_tiles, F // tf),
        in_specs=[pl.BlockSpec(memory_space=pl.ANY),
                  pl.BlockSpec((1, D), lambda t, f, te, ns, sr: (0, 0)),
                  pl.BlockSpec((tg, 1), lambda t, f, te, ns, sr: (t, 0)),
                  pl.BlockSpec((1, 1, D, tf), lambda t, f, te, ns, sr: (layer, te[t], 0, f)),
                  pl.BlockSpec((1, 1, D, tf), lambda t, f, te, ns, sr: (layer, te[t], 0, f)),
                  pl.BlockSpec((1, 1, tf, D), lambda t, f, te, ns, sr: (layer, te[t], f, 0))],
        out_specs=pl.BlockSpec((tg * SUBLANES, LANES), lambda t, f, te, ns, sr: (t, 0)),
        scratch_shapes=[pltpu.VMEM((2, tg * SUBLANES, LANES), F32),
                        pltpu.SemaphoreType.DMA((2,)),
                        pltpu.VMEM((tg, D), BF16),
                        pltpu.VMEM((tg, D), F32)])
    y2d = pl.pallas_call(
        functools.partial(_moe_group_kernel, tg=tg, sub=sub),
        grid_spec=grid_spec,
        out_shape=jax.ShapeDtypeStruct((P * SUBLANES, LANES), F32),
        compiler_params=_cparams(2),
        name="moe_group",
    )(tile_e, n_sub, src, x2d, g.reshape(1, D), gate_sorted, w1, w3, w2)

    tc = _pick_tile(M, (512, 256, 128))
    pos2 = pos.reshape(M, TOP_K)
    blk = pl.BlockSpec((tc, D), lambda i, p0, p1: (i, 0))
    return pl.pallas_call(
        functools.partial(_moe_combine_kernel, tc=tc),
        grid_spec=pltpu.PrefetchScalarGridSpec(
            num_scalar_prefetch=2,
            grid=(M // tc,),
            in_specs=[blk, pl.BlockSpec(memory_space=pl.ANY)],
            out_specs=blk,
            scratch_shapes=[pltpu.VMEM((tc * SUBLANES, LANES), F32),
                            pltpu.VMEM((tc * SUBLANES, LANES), F32),
                            pltpu.SemaphoreType.DMA((2,))]),
        out_shape=jax.ShapeDtypeStruct((M, D), F32),
        compiler_params=_cparams(1),
        name="moe_combine",
    )(pos2[:, 0], pos2[:, 1], x, y2d)


def _rope_head(xh, tc, ts1, ts2):
    return xh * tc + pltpu.roll(xh, 16, 1) * ts1 + pltpu.roll(xh, LANES - 16, 1) * ts2


def _mla_prep_kernel(cq_ref, kr_ref, ckv_ref, tc_ref, ts1_ref, ts2_ref,
                     qag_ref, kvg_ref, qng_ref, kng_ref, wuq_ref, wuk_ref, wuv_ref,
                     q_ref, k_ref, v_ref, rows_ref, *, precise):
    tc, ts1, ts2 = tc_ref[...], ts1_ref[...], ts2_ref[...]
    cq = cq_ref[...]
    cqn = cq * lax.rsqrt(jnp.mean(cq * cq, axis=-1, keepdims=True) + EPS) * qag_ref[...]
    qraw = _mm(cqn, wuq_ref[...], precise)
    ckv = ckv_ref[...]
    lat = ckv * lax.rsqrt(jnp.mean(ckv * ckv, axis=-1, keepdims=True) + EPS) * kvg_ref[...]
    kraw = _mm(lat, wuk_ref[...], precise)
    v_ref[...] = _mm(lat, wuv_ref[...], precise).astype(v_ref.dtype)
    krr = _rope_head(pltpu.roll(kr_ref[...], QK_NOPE, 1), tc, ts1, ts2)
    lane = lax.broadcasted_iota(jnp.int32, krr.shape, 1)
    krr = jnp.where(jnp.logical_and(lane >= QK_NOPE, lane < QK_HEAD), krr, 0.0)
    rows_ref[:, :KV_LORA] = lat
    rows_ref[:, KV_LORA:] = pltpu.roll(krr, LANES - QK_NOPE, 1)[:, :QK_ROPE]
    qng = qng_ref[...]
    kng = kng_ref[...]
    for h in range(MLA_HEADS):
        sl = slice(h * HEAD_PAD, (h + 1) * HEAD_PAD)
        qh = _rope_head(qraw[:, sl], tc, ts1, ts2)
        qs = jnp.sum(qh * qh, axis=-1, keepdims=True) * (1.0 / QK_HEAD)
        q_ref[:, sl] = (qh * lax.rsqrt(qs + EPS) * qng).astype(q_ref.dtype)
        kh = kraw[:, sl] + krr
        ks = jnp.sum(kh * kh, axis=-1, keepdims=True) * (1.0 / QK_HEAD)
        k_ref[:, sl] = (kh * lax.rsqrt(ks + EPS) * kng).astype(k_ref.dtype)


def mla_prep(z, tabs, qag, kvg, qng, kng, wuq, wuk, wuv, precise=False):
    M = z.shape[0]
    tm = _pick_tile(M, (512, 256, 128))
    HP = MLA_HEADS * HEAD_PAD
    odt = F32 if precise else BF16
    col = lambda off, w: pl.BlockSpec((tm, w), lambda i: (i, off // w))
    row = lambda w: pl.BlockSpec((1, w), lambda i: (0, 0))
    full = lambda a: pl.BlockSpec(a.shape, lambda i: (0, 0))
    tab = pl.BlockSpec((tm, LANES), lambda i: (i, 0))
    return pl.pallas_call(
        functools.partial(_mla_prep_kernel, precise=precise),
        grid=(M // tm,),
        in_specs=[col(Z_CQ, Q_LORA), col(Z_KR, LANES), col(Z_CKV, KV_LORA), tab, tab, tab,
                  row(Q_LORA), row(KV_LORA), row(LANES), row(LANES),
                  full(wuq), full(wuk), full(wuv)],
        out_specs=[pl.BlockSpec((tm, HP), lambda i: (i, 0)),
                   pl.BlockSpec((tm, HP), lambda i: (i, 0)),
                   pl.BlockSpec((tm, MLA_HEADS * V_HEAD), lambda i: (i, 0)),
                   pl.BlockSpec((tm, LAT_W), lambda i: (i, 0))],
        out_shape=[jax.ShapeDtypeStruct((M, HP), odt),
                   jax.ShapeDtypeStruct((M, HP), odt),
                   jax.ShapeDtypeStruct((M, MLA_HEADS * V_HEAD), odt),
                   jax.ShapeDtypeStruct((M, LAT_W), F32)],
        compiler_params=_cparams(1),
        name="mla_prep",
    )(z, z, z, *tabs, qag, kvg, qng, kng, wuq, wuk, wuv)


def _attn_prompt_kernel(q_ref, k_ref, v_ref, o_ref, *, tq, heads_per_loop):
    i = pl.program_id(1)
    rid = lax.broadcasted_iota(jnp.int32, (tq, tq), 0)
    cid = lax.broadcasted_iota(jnp.int32, (tq, tq), 1)
    causal = cid <= rid
    ksl = lambda h: slice(h * HEAD_PAD, (h + 1) * HEAD_PAD)
    vsl = lambda h: slice(h * V_HEAD, (h + 1) * V_HEAD)

    def step(s, vblk, carry):
        m, l, acc = carry
        m_new = jnp.maximum(m, jnp.max(s, axis=-1, keepdims=True))
        p = jnp.exp(s - m_new)
        alpha = jnp.exp(m - m_new)
        l_new = alpha * l + jnp.sum(p, axis=-1, keepdims=True)
        acc_new = alpha * acc + _dot(p.astype(BF16), vblk)
        return m_new, l_new, acc_new

    for h0 in range(0, MLA_HEADS, heads_per_loop):
        heads = tuple(range(h0, h0 + heads_per_loop))
        qs = [q_ref[:, ksl(h)] for h in heads]

        def body(j, carry, heads=heads, qs=qs):
            r0 = pl.multiple_of(j * tq, tq)
            return tuple(step(_dot_nt(qs[n], k_ref[pl.ds(r0, tq), ksl(h)]), v_ref[pl.ds(r0, tq), vsl(h)], carry[n])
                         for n, h in enumerate(heads))

        init = tuple((jnp.full((tq, 1), NEG, F32), jnp.zeros((tq, 1), F32), jnp.zeros((tq, V_HEAD), F32))
                     for _ in heads)
        carry = lax.fori_loop(0, i, body, init)
        r0 = pl.multiple_of(i * tq, tq)
        for n, h in enumerate(heads):
            s = jnp.where(causal, _dot_nt(qs[n], k_ref[pl.ds(r0, tq), ksl(h)]), NEG)
            m, l, acc = step(s, v_ref[pl.ds(r0, tq), vsl(h)], carry[n])
            o_ref[:, vsl(h)] = (acc / l).astype(BF16)


def attn_prompt(q, k, v, B, TP):
    tq = _pick_tile(TP, (384, 256, 128))
    nq = TP // tq
    HP = MLA_HEADS * HEAD_PAD
    VW = MLA_HEADS * V_HEAD
    return pl.pallas_call(
        functools.partial(_attn_prompt_kernel, tq=tq, heads_per_loop=4),
        grid=(B, nq),
        in_specs=[pl.BlockSpec((tq, HP), lambda b, i: (b * nq + i, 0)),
                  pl.BlockSpec((TP, HP), lambda b, i: (b, 0)),
                  pl.BlockSpec((TP, VW), lambda b, i: (b, 0))],
        out_specs=pl.BlockSpec((tq, VW), lambda b, i: (b * nq + i, 0)),
        out_shape=jax.ShapeDtypeStruct((B * TP, VW), BF16),
        compiler_params=_cparams(2),
        name="attn_prompt",
    )(q, k, v)


def _attn_sample_kernel(pt_ref, qn_ref, qr_ref, nrow_ref, cache_ref, wukt_ref, wuv_ref,
                        o_ref, buf_ref, sem_ref, l_ref, kr_ref, sc_ref, latb_ref,
                        *, layer, n_pages, chunk):
    b = pl.program_id(0)
    nb = pl.num_programs(0)
    ppc = chunk // PAGE
    n_chunks = n_pages // ppc
    slot = b % 2
    H = MLA_HEADS
    NK = MLA_HEADS * QK_NOPE

    def page_copy(seq, p, s):
        return pltpu.make_async_copy(cache_ref.at[layer, pt_ref[seq, p]],
                                     buf_ref.at[s, p // ppc, :, pl.ds((p % ppc) * PAGE, PAGE)],
                                     sem_ref.at[s])

    def start_all(seq, s):
        for p in range(n_pages):
            page_copy(seq, p, s).start()

    @pl.when(b == 0)
    def _():
        start_all(0, 0)
        l_ref[:NK, :] = wukt_ref[...]
        kr_ref[...] = jnp.zeros_like(kr_ref)

    @pl.when(b + 1 < nb)
    def _():
        start_all(b + 1, 1 - slot)

    l_ref[NK:, :] = _dot(qn_ref[0], wukt_ref[...]).astype(BF16)
    qr = qr_ref[0]
    hid = lax.broadcasted_iota(jnp.int32, (H, 1), 0)

    def scores(rows_t, n):
        latb = rows_t[:KV_LORA, :].astype(BF16)
        kr = rows_t[KV_LORA:, :]
        kr_ref[:QK_ROPE, :n] = kr
        kt = _dot(l_ref[...], latb)
        raw = kt[NK:NK + H, :] + _dot(qr, kr_ref[:, :n].astype(BF16))[:H]
        krsq = jnp.sum(kr * kr, axis=0, keepdims=True)
        ss = jnp.zeros((H, n), F32)
        for h in range(H):
            blk = kt[h * QK_NOPE:(h + 1) * QK_NOPE, :]
            ss = ss + jnp.where(hid == h, jnp.sum(blk * blk, axis=0, keepdims=True), 0.0)
        return raw * lax.rsqrt((ss + krsq) * (1.0 / QK_HEAD) + EPS), latb

    sc_new, latb_new = scores(nrow_ref[0], PAGE)
    sc_new = jnp.where(lax.broadcasted_iota(jnp.int32, sc_new.shape, 1) == 0, sc_new, NEG)

    for p in range(n_pages):
        page_copy(b, p, slot).wait()

    def chunk_body(c, _):
        sc, latb = scores(buf_ref[slot, c], chunk)
        sc_ref[c] = sc
        latb_ref[c] = latb
        return 0

    lax.fori_loop(0, n_chunks, chunk_body, 0)

    sc = sc_ref[...]
    mx = jnp.maximum(jnp.max(jnp.max(sc, axis=-1, keepdims=True), axis=0),
                     jnp.max(sc_new, axis=-1, keepdims=True))
    p_new = jnp.exp(sc_new - mx)
    l = jnp.sum(p_new, axis=-1, keepdims=True)
    zpad = jnp.zeros((H, PAGE), F32)
    o_lat = _dot_nt(jnp.concatenate([p_new, zpad], axis=0).astype(BF16), latb_new)[:H]
    zpad = jnp.zeros((H, chunk), F32)
    for c in range(n_chunks):
        pc = jnp.exp(sc[c] - mx)
        l = l + jnp.sum(pc, axis=-1, keepdims=True)
        o_lat = o_lat + _dot_nt(jnp.concatenate([pc, zpad], axis=0).astype(BF16), latb_ref[c])[:H]
    o_lat = o_lat / l
    full = _dot(jnp.concatenate([o_lat, jnp.zeros_like(o_lat)], axis=0).astype(BF16), wuv_ref[...])[:H]
    cid = lax.broadcasted_iota(jnp.int32, full.shape, 1)
    rid = lax.broadcasted_iota(jnp.int32, full.shape, 0)
    o_ref[0] = jnp.sum(jnp.where(cid // V_HEAD == rid, full, 0.0), axis=0, keepdims=True)


def attn_sample(qn, qr, new_rows, cache_t, layer, page_table, wukt, wuv):
    Bs, n_pages = page_table.shape
    t_past = n_pages * PAGE
    chunk = _pick_tile(t_past, (1024, 512, 256, 128))
    n_chunks = t_past // chunk
    H2 = 2 * MLA_HEADS
    NK = MLA_HEADS * QK_NOPE
    VW = MLA_HEADS * V_HEAD
    new_t = jnp.pad(new_rows[:, :, None], ((0, 0), (0, 0), (0, PAGE - 1)))
    grid_spec = pltpu.PrefetchScalarGridSpec(
        num_scalar_prefetch=1,
        grid=(Bs,),
        in_specs=[pl.BlockSpec((1, H2, NK), lambda b, pt: (b, 0, 0)),
                  pl.BlockSpec((1, H2, LANES), lambda b, pt: (b, 0, 0)),
                  pl.BlockSpec((1, LAT_W, PAGE), lambda b, pt: (b, 0, 0)),
                  pl.BlockSpec(memory_space=pl.ANY),
                  pl.BlockSpec(wukt.shape, lambda b, pt: (0, 0)),
                  pl.BlockSpec(wuv.shape, lambda b, pt: (0, 0))],
        out_specs=pl.BlockSpec((1, 1, VW), lambda b, pt: (b, 0, 0)),
        scratch_shapes=[pltpu.VMEM((2, n_chunks, LAT_W, chunk), F32),
                        pltpu.SemaphoreType.DMA((2,)),
                        pltpu.VMEM((NK + H2, KV_LORA), BF16),
                        pltpu.VMEM((LANES, chunk), F32),
                        pltpu.VMEM((n_chunks, MLA_HEADS, chunk), F32),
                        pltpu.VMEM((n_chunks, KV_LORA, chunk), BF16)])
    out = pl.pallas_call(
        functools.partial(_attn_sample_kernel, layer=layer, n_pages=n_pages, chunk=chunk),
        grid_spec=grid_spec,
        out_shape=jax.ShapeDtypeStruct((Bs, 1, VW), F32),
        compiler_params=_cparams(1),
        name="attn_sample",
    )(page_table, qn, qr, new_t, cache_t, wukt, wuv)
    return out.reshape(Bs, VW)


def _mlstm_prompt_kernel(u_ref, v_ref, og_ref, gt_ref, cw_ref, cb_ref, wq_ref, wk_ref, gb_ref, ng_ref,
                         hm_ref, c_out, n_out, m_out,
                         uprev_ref, cs_ref, ns_ref, ms_ref, *, t_valid):
    L = SEQ_BLOCK
    i = pl.program_id(0)

    @pl.when(i == 0)
    def _():
        uprev_ref[...] = jnp.zeros_like(uprev_ref)
        cs_ref[...] = jnp.zeros_like(cs_ref)
        ns_ref[...] = jnp.zeros_like(ns_ref)
        ms_ref[...] = jnp.zeros_like(ms_ref)

    cw = cw_ref[...]
    row = lax.broadcasted_iota(jnp.int32, (L, 1), 0) + i * L
    valid = row < t_valid
    rid = lax.broadcasted_iota(jnp.int32, (L, L), 0)
    cid = lax.broadcasted_iota(jnp.int32, (L, L), 1)
    causal = cid <= rid
    tri = causal.astype(BF16)

    for bi, h in [(bi, h) for bi in range(u_ref.shape[0]) for h in range(ML_HEADS)]:
        if h == 0:
            u = u_ref[bi]
            prev8 = uprev_ref[bi]
            uc = cb_ref[...] + cw[3:4] * u
            for k in range(1, CONV_W):
                uc = uc + cw[3 - k:4 - k] * _shift_rows(u, prev8, k)
            uprev_ref[bi] = u[L - SUBLANES:, :]
            uc = uc * _sigmoid(uc)
            gb = gt_ref[bi] + gb_ref[...]
            li_all = jnp.where(valid, gb, NEG)
            lf_all = jnp.where(valid, _log_sigmoid(gb), 0.0)
            lf_hi, lf_lo = _split_bf16(lf_all)
            b_all = _dot(tri, lf_hi) + _dot(tri, lf_lo)
            li_t = li_all.T
            b_t = b_all.T
        sl = slice(h * ML_HEAD, (h + 1) * ML_HEAD)
        uh = uc[:, sl].astype(BF16)
        qh = _dot(uh, wq_ref[h])
        kh = _dot(uh, wk_ref[h]) * (ML_HEAD ** -0.5)
        vh = v_ref[bi, :, sl]
        qb, kb = qh.astype(BF16), kh.astype(BF16)
        li_col = li_all[:, h:h + 1]
        li_row = li_t[h:h + 1, :]
        b_col = b_all[:, ML_HEADS + h:ML_HEADS + h + 1]
        b_row = b_t[ML_HEADS + h:ML_HEADS + h + 1, :]
        m_prev = ms_ref[bi, h:h + 1, 0:1]
        n_row = ns_ref[bi, h:h + 1, :]
        c_prev = cs_ref[bi, h]

        dm = jnp.where(causal, b_col - b_row + li_row, NEG)
        inter = b_col + m_prev
        m_t = jnp.maximum(inter, jnp.max(dm, axis=-1, keepdims=True))
        w_intra = jnp.exp(dm - m_t)
        w_inter = jnp.exp(inter - m_t)
        s = _dot_nt(qb, kb) * w_intra
        num = _dot(s.astype(BF16), vh.astype(BF16)) + w_inter * _dot_nt(qb, c_prev.astype(BF16))
        den = jnp.sum(s, axis=-1, keepdims=True) + w_inter * jnp.sum(qh * n_row, axis=-1, keepdims=True)
        hh = num / jnp.maximum(jnp.abs(den), jnp.exp(-m_t))

        b_end = b_col[L - 1:L, :]
        g_col = b_end - b_col + li_col
        g_row = b_end - b_row + li_row
        m_new = jnp.maximum(b_end + m_prev, jnp.max(g_row, axis=-1, keepdims=True))
        wg_col = jnp.exp(g_col - m_new)
        wc = jnp.exp(b_end + m_prev - m_new)
        cs_ref[bi, h] = wc * c_prev + _dot_tn((vh * wg_col).astype(BF16), kb)
        ns_ref[bi, h:h + 1, :] = wc * n_row + jnp.sum(kh * wg_col, axis=0, keepdims=True)
        ms_ref[bi, h:h + 1, :] = jnp.broadcast_to(m_new, (1, LANES))

        hn = hh * lax.rsqrt(jnp.mean(hh * hh, axis=-1, keepdims=True) + EPS) * ng_ref[h:h + 1, :]
        hm_ref[bi, :, sl] = (hn * _sigmoid(og_ref[bi, :, sl])).astype(BF16)

    @pl.when(i == pl.num_programs(0) - 1)
    def _():
        c_out[...] = cs_ref[...]
        n_out[...] = ns_ref[:, :ML_HEADS, :]
        m_out[...] = ms_ref[:, :ML_HEADS, :]


def mlstm_prompt(z, cw, cb, wq, wk, gate_bias, ng, B, TP, t_valid):
    nc = TP // SEQ_BLOCK
    z3 = z.reshape(B, TP, z.shape[1])
    zcol = lambda off, w: pl.BlockSpec((B, SEQ_BLOCK, w), lambda i: (0, i, off // w))
    full = lambda a: pl.BlockSpec(a.shape, lambda i: (0,) * a.ndim)
    hm, c_p, n_p, m_p = pl.pallas_call(
        functools.partial(_mlstm_prompt_kernel, t_valid=t_valid),
        grid=(nc,),
        in_specs=[zcol(Z_U, ML_W), zcol(Z_V, ML_W), zcol(Z_OG, ML_W), zcol(Z_GT, LANES),
                  full(cw), full(cb), full(wq), full(wk), full(gate_bias), full(ng)],
        out_specs=[pl.BlockSpec((B, SEQ_BLOCK, ML_W), lambda i: (0, i, 0)),
                   pl.BlockSpec((B, ML_HEADS, ML_HEAD, ML_HEAD), lambda i: (0, 0, 0, 0)),
                   pl.BlockSpec((B, ML_HEADS, ML_HEAD), lambda i: (0, 0, 0)),
                   pl.BlockSpec((B, ML_HEADS, LANES), lambda i: (0, 0, 0))],
        out_shape=[jax.ShapeDtypeStruct((B, TP, ML_W), BF16),
                   jax.ShapeDtypeStruct((B, ML_HEADS, ML_HEAD, ML_HEAD), F32),
                   jax.ShapeDtypeStruct((B, ML_HEADS, ML_HEAD), F32),
                   jax.ShapeDtypeStruct((B, ML_HEADS, LANES), F32)],
        scratch_shapes=[pltpu.VMEM((B, SUBLANES, ML_W), F32),
                        pltpu.VMEM((B, ML_HEADS, ML_HEAD, ML_HEAD), F32),
                        pltpu.VMEM((B, SUBLANES, ML_HEAD), F32),
                        pltpu.VMEM((B, SUBLANES, LANES), F32)],
        compiler_params=_cparams(1),
        name="mlstm_prompt",
    )(z3, z3, z3, z3, cw, cb, wq, wk, gate_bias, ng)
    return hm.reshape(B * TP, ML_W), c_p, n_p, m_p


def _mlstm_sample_kernel(u_ref, v_ref, og_ref, gt_ref, buf_ref, c_ref, n_ref, m_ref,
                         cw_ref, cb_ref, wq_ref, wk_ref, gb_ref, ng_ref,
                         hm_ref, c_out, n_out, m_out, buf_out):
    u = u_ref[0]
    buf = buf_ref[0]
    cw = cw_ref[...]
    uc = cb_ref[...] + cw[3:4] * u
    for j in range(CONV_W - 1):
        uc = uc + cw[j:j + 1] * buf[j:j + 1]
    buf_out[0, 0:CONV_W - 2, :] = buf[1:CONV_W - 1]
    buf_out[0, CONV_W - 2:CONV_W - 1, :] = u
    uc = uc * _sigmoid(uc)
    gb = gt_ref[0] + gb_ref[...]
    lane = lax.broadcasted_iota(jnp.int32, (1, LANES), 1)
    m_all = m_ref[0]
    m_acc = jnp.zeros((1, LANES), F32)
    for h in range(ML_HEADS):
        sl = slice(h * ML_HEAD, (h + 1) * ML_HEAD)
        uh = jnp.broadcast_to(uc[:, sl], (ROWS16, ML_HEAD))
        qh = _dot3(uh, wq_ref[h])
        kh = _dot3(uh, wk_ref[h]) * (ML_HEAD ** -0.5)
        vh = v_ref[0][:, sl]
        li = gb[:, h:h + 1]
        lf = _log_sigmoid(gb[:, ML_HEADS + h:ML_HEADS + h + 1])
        m_prev = m_all[:, h:h + 1]
        c_prev = c_ref[0, h]
        n_row = n_ref[0, h:h + 1, :]
        m_t = jnp.maximum(lf + m_prev, li)
        w_in = jnp.exp(li - m_t)
        w_st = jnp.exp(lf + m_prev - m_t)
        s = jnp.sum(qh[0:1] * kh[0:1], axis=-1, keepdims=True) * w_in
        cq = _dot3(qh, c_prev, _dot_nt)[0:1]
        num = s * vh + w_st * cq
        den = s + w_st * jnp.sum(qh[0:1] * n_row, axis=-1, keepdims=True)
        hh = num / jnp.maximum(jnp.abs(den), jnp.exp(-m_t))
        v16 = jnp.where(lax.broadcasted_iota(jnp.int32, (ROWS16, ML_HEAD), 0) == 0,
                        jnp.broadcast_to(vh * w_in, (ROWS16, ML_HEAD)), 0.0)
        c_out[0, h] = w_st * c_prev + _dot3(v16, kh, _dot_tn)
        n_out[0, h:h + 1, :] = w_st * n_row + w_in * kh[0:1]
        m_acc = m_acc + jnp.where(lane == h, m_t, 0.0)
        hn = hh * lax.rsqrt(jnp.mean(hh * hh, axis=-1, keepdims=True) + EPS) * ng_ref[h:h + 1, :]
        hm_ref[0, :, sl] = hn * _sigmoid(og_ref[0][:, sl])
    m_out[0] = m_acc


def mlstm_sample(zs, buf, C, n, m, cw, cb, wq, wk, gate_bias, ng):
    Bs = zs.shape[0]
    mp = jnp.pad(m, ((0, 0), (0, LANES - ML_HEADS))).reshape(Bs, 1, LANES)
    zcol = lambda off, w: pl.BlockSpec((1, 1, w), lambda b: (b, 0, off // w))
    full = lambda a: pl.BlockSpec(a.shape, lambda b: (0,) * a.ndim)
    seq = lambda a: pl.BlockSpec((1,) + a.shape[1:], lambda b: (b,) + (0,) * (a.ndim - 1))
    outs = pl.pallas_call(
        _mlstm_sample_kernel,
        grid=(Bs,),
        in_specs=[zcol(Z_U, ML_W), zcol(Z_V, ML_W), zcol(Z_OG, ML_W), zcol(Z_GT, LANES),
                  seq(buf), seq(C), seq(n), seq(mp),
                  full(cw), full(cb), full(wq), full(wk), full(gate_bias), full(ng)],
        out_specs=[pl.BlockSpec((1, 1, ML_W), lambda b: (b, 0, 0)),
                   seq(C), seq(n), seq(mp), seq(buf)],
        out_shape=[jax.ShapeDtypeStruct((Bs, 1, ML_W), F32),
                   jax.ShapeDtypeStruct(C.shape, F32),
                   jax.ShapeDtypeStruct(n.shape, F32),
                   jax.ShapeDtypeStruct(mp.shape, F32),
                   jax.ShapeDtypeStruct(buf.shape, F32)],
        compiler_params=_cparams(1),
        name="mlstm_sample",
    )(zs, zs, zs, zs, buf, C, n, mp, cw, cb, wq, wk, gate_bias, ng)
    hm, C_new, n_new, m_new, buf_new = outs
    return hm.reshape(Bs, ML_W), C_new, n_new, m_new.reshape(Bs, LANES)[:, :ML_HEADS], buf_new


def _lru_gates(xc, wa_ref, wx_ref, ba, bx, c, precise=False):
    rs, iss = [], []
    for g in range(LRU_BLOCKS):
        xg = xc[:, g * LANES:(g + 1) * LANES]
        rs.append(_mm(xg, wa_ref[g], precise))
        iss.append(_mm(xg, wx_ref[g], precise))
    r = _sigmoid(jnp.concatenate(rs, axis=-1) + ba)
    ig = _sigmoid(jnp.concatenate(iss, axis=-1) + bx)
    a = jnp.exp(c * r)
    return a, jnp.sqrt(1.0 - a * a) * ig * xc


def _rglru_prompt_kernel(gate_ref, xb_ref, cw_ref, cb_ref, wa_ref, wx_ref, ba_ref, bx_ref, c_ref,
                         y_ref, h_out, xprev_ref, hs_ref, *, t_valid):
    T = SEQ_BLOCK
    i = pl.program_id(1)

    @pl.when(i == 0)
    def _():
        xprev_ref[...] = jnp.zeros_like(xprev_ref)
        hs_ref[...] = jnp.zeros_like(hs_ref)

    xb = xb_ref[...]
    prev8 = xprev_ref[...]
    cw = cw_ref[...]
    xc = cb_ref[...] + cw[3:4] * xb
    for k in range(1, CONV_W):
        xc = xc + cw[3 - k:4 - k] * _shift_rows(xb, prev8, k)
    xprev_ref[...] = xb[T - SUBLANES:, :]

    a, bx = _lru_gates(xc, wa_ref, wx_ref, ba_ref[...], bx_ref[...], c_ref[...])
    row = lax.broadcasted_iota(jnp.int32, (T, 1), 0)
    valid = row + i * T < t_valid
    a = jnp.where(valid, a, 1.0)
    bx = jnp.where(valid, bx, 0.0)
    k = 1
    while k < T:
        keep = row >= k
        a_sh = jnp.where(keep, pltpu.roll(a, k, 0), 1.0)
        b_sh = jnp.where(keep, pltpu.roll(bx, k, 0), 0.0)
        bx = a * b_sh + bx
        a = a * a_sh
        k *= 2
    hs = a * hs_ref[0:1, :] + bx
    hs_ref[...] = jnp.broadcast_to(hs[T - 1:T, :], hs_ref.shape)
    y_ref[...] = (hs * jax.nn.gelu(gate_ref[...])).astype(BF16)

    @pl.when(i == pl.num_programs(1) - 1)
    def _():
        h_out[0] = hs[T - 1:T, :]


def rglru_prompt(zz, cw, cb, wa, wx, ba, bx, c, B, TP, t_valid):
    W = zz.shape[1] // 2
    nb = TP // SEQ_BLOCK
    full = lambda a: pl.BlockSpec(a.shape, lambda b, i: (0,) * a.ndim)
    return pl.pallas_call(
        functools.partial(_rglru_prompt_kernel, t_valid=t_valid),
        grid=(B, nb),
        in_specs=[pl.BlockSpec((SEQ_BLOCK, W), lambda b, i: (b * nb + i, 0)),
                  pl.BlockSpec((SEQ_BLOCK, W), lambda b, i: (b * nb + i, 1)),
                  full(cw), full(cb), full(wa), full(wx), full(ba), full(bx), full(c)],
        out_specs=[pl.BlockSpec((SEQ_BLOCK, W), lambda b, i: (b * nb + i, 0)),
                   pl.BlockSpec((1, 1, W), lambda b, i: (b, 0, 0))],
        out_shape=[jax.ShapeDtypeStruct((B * TP, W), BF16),
                   jax.ShapeDtypeStruct((B, 1, W), F32)],
        scratch_shapes=[pltpu.VMEM((SUBLANES, W), F32), pltpu.VMEM((SUBLANES, W), F32)],
        compiler_params=_cparams(2),
        name="rglru_prompt",
    )(zz, zz, cw, cb, wa, wx, ba, bx, c)


def _rglru_sample_kernel(gate_ref, xb_ref, b0_ref, b1_ref, b2_ref, h0_ref,
                         cw_ref, cb_ref, wa_ref, wx_ref, ba_ref, bx_ref, c_ref,
                         y_ref, h_out):
    xb = xb_ref[...]
    cw = cw_ref[...]
    xc = cb_ref[...] + cw[0:1] * b0_ref[...] + cw[1:2] * b1_ref[...] + cw[2:3] * b2_ref[...] + cw[3:4] * xb
    a, bx = _lru_gates(xc, wa_ref, wx_ref, ba_ref[...], bx_ref[...], c_ref[...], precise=True)
    hs = a * h0_ref[...] + bx
    h_out[...] = hs
    y_ref[...] = hs * jax.nn.gelu(gate_ref[...])


def rglru_sample(gate, xb, buf, h0, cw, cb, wa, wx, ba, bx, c):
    Bs, W = xb.shape
    args = (gate, xb, buf[:, 0], buf[:, 1], buf[:, 2], h0, cw, cb, wa, wx, ba, bx, c)
    full = lambda a: pl.BlockSpec(a.shape, lambda i: (0,) * a.ndim)
    return pl.pallas_call(
        _rglru_sample_kernel,
        grid=(1,),
        in_specs=[full(a) for a in args],
        out_specs=[pl.BlockSpec((Bs, W), lambda i: (0, 0)), pl.BlockSpec((Bs, W), lambda i: (0, 0))],
        out_shape=[jax.ShapeDtypeStruct((Bs, W), F32), jax.ShapeDtypeStruct((Bs, W), F32)],
        compiler_params=_cparams(1),
        name="rglru_sample",
    )(*args)


def _pad_cols(w, n):
    return jnp.pad(w, ((0, 0), (0, n - w.shape[1])))


def _even_in_proj(w):
    o = 0
    pieces = {}
    for name, width in (("cq", Q_LORA), ("ckv", KV_LORA), ("kr", QK_ROPE), ("u", ML_W), ("v", ML_W),
                        ("og", ML_W), ("ig", ML_HEADS), ("fg", ML_HEADS)):
        pieces[name] = w[:, o:o + width]
        o += width
    gates = _pad_cols(jnp.concatenate([pieces["ig"], pieces["fg"]], axis=1), Z_W - Z_GT)
    return jnp.concatenate([pieces["cq"], _pad_cols(pieces["kr"], LANES), pieces["u"], pieces["v"],
                            pieces["og"], pieces["ckv"], gates], axis=1)


def _head_pad(w):
    K = w.shape[0]
    d = w.shape[1] // MLA_HEADS
    w3 = w.reshape(K, MLA_HEADS, d)
    return jnp.pad(w3, ((0, 0), (0, 0), (0, HEAD_PAD - d))).reshape(K, MLA_HEADS * HEAD_PAD)


def _rope_tables(pos):
    half = QK_ROPE // 2
    inv_freq = ROPE_THETA ** (-jnp.arange(half, dtype=F32) / half)
    ang = pos.astype(F32)[:, None] * inv_freq[None, :]
    cos, sin = jnp.cos(ang), jnp.sin(ang)
    n = pos.shape[0]
    one = jnp.ones((n, QK_NOPE), F32)
    zh = jnp.zeros((n, half), F32)
    zn = jnp.zeros((n, QK_NOPE), F32)
    zt = jnp.zeros((n, HEAD_PAD - QK_HEAD), F32)
    tc = jnp.concatenate([one, cos, cos, zt], axis=1)
    ts1 = jnp.concatenate([zn, zh, sin, zt], axis=1)
    ts2 = jnp.concatenate([zn, -sin, zh, zt], axis=1)
    return tc, ts1, ts2


def kernel(x_prompt, x_sample, cache_mla, state_mlstm_C, state_mlstm_n, state_mlstm_m, state_mlstm_conv, state_rglru_h, state_rglru_conv, page_table, meta_tokens, norm_mix_even, w_in_even, mla_q_a_g, mla_w_uq, mla_kv_a_g, mla_w_uk, mla_w_uv, mla_q_norm_g, mla_k_norm_g, ml_conv_w, ml_conv_b, ml_w_q, ml_w_k, ml_b_i, ml_b_f, ml_norm_g, w_out_even, norm_ffn_even, ffn_w1, ffn_w3, ffn_w2, norm_mix_odd, w_in_odd, lru_conv_w, lru_conv_b, lru_w_a, lru_b_a, lru_w_x, lru_b_x, lru_lam, w_out_odd, norm_ffn_odd, moe_router, moe_w1, moe_w3, moe_w2):
    B, S, D = x_prompt.shape
    Bs = x_sample.shape[0]
    assert x_sample.shape[1] == 1
    Tp = N_META + S
    TP = _round_up(Tp, SEQ_BLOCK)
    n_prompt = B * TP
    past_len = page_table.shape[1] * PAGE
    depth = norm_mix_even.shape[0] + norm_mix_odd.shape[0]

    meta = jnp.broadcast_to(meta_tokens[None], (B, N_META, D))
    xp = jnp.pad(jnp.concatenate([meta, x_prompt], axis=1), ((0, 0), (0, TP - Tp), (0, 0)))
    x = xp.reshape(n_prompt, D)
    xs = x_sample.reshape(Bs, D)
    tabs = _rope_tables(jnp.tile(jnp.arange(TP, dtype=jnp.int32), B))
    tabs_s = _rope_tables(jnp.full((Bs,), past_len, jnp.int32))
    bf = lambda a: a.astype(BF16)
    cache_t = jnp.swapaxes(cache_mla, 2, 3)
    ffn_w = tuple(w[:, None] for w in (ffn_w1, ffn_w3, ffn_w2))
    ffn_bf = tuple(bf(w) for w in ffn_w)
    moe_bf = tuple(bf(w) for w in (moe_w1, moe_w3, moe_w2))

    def prompt_view(a):
        return a.reshape((B, TP) + a.shape[1:])

    rows_p, rows_s, Cp, Cs, np_, ns, mp, ms, cbp, cbs = [], [], [], [], [], [], [], [], [], []
    hp_l, hs_l, lbp, lbs = [], [], [], []
    for layer in range(depth):
        if layer % 2 == 0:
            e = layer // 2
            w_in = _even_in_proj(w_in_even[e])
            z = norm_matmul(x, norm_mix_even[e], bf(w_in))
            zs = norm_matmul(xs, norm_mix_even[e], w_in, precise=True)
            qng = jnp.pad(mla_q_norm_g[e] * (QK_HEAD ** -0.5), (0, HEAD_PAD - QK_HEAD)).reshape(1, HEAD_PAD)
            kng = jnp.pad(mla_k_norm_g[e], (0, HEAD_PAD - QK_HEAD)).reshape(1, HEAD_PAD)
            wuk2 = mla_w_uk[e].reshape(KV_LORA, MLA_HEADS * QK_NOPE)
            wuv2 = mla_w_uv[e].reshape(KV_LORA, MLA_HEADS * V_HEAD)
            prep_g = (mla_q_a_g[e].reshape(1, Q_LORA), mla_kv_a_g[e].reshape(1, KV_LORA), qng, kng)
            prep_w = (_head_pad(mla_w_uq[e]), _head_pad(wuk2), wuv2)
            q, k, v, rows = mla_prep(z, tabs, *prep_g, *(bf(w) for w in prep_w))
            q_s, _, _, rows_sample = mla_prep(zs, tabs_s, *prep_g, *prep_w, precise=True)
            att_p = attn_prompt(q, k, v, B, TP)
            qs = q_s.reshape(Bs, MLA_HEADS, HEAD_PAD) * kng.reshape(1, 1, HEAD_PAD)
            eye = jnp.eye(MLA_HEADS, dtype=F32)
            qn_bd = (qs[:, :, None, :QK_NOPE] * eye[None, :, :, None]).reshape(Bs, MLA_HEADS, MLA_HEADS * QK_NOPE)
            qr_p = jnp.pad(qs[:, :, QK_NOPE:QK_HEAD], ((0, 0), (0, 0), (0, LANES - QK_ROPE)))
            pad_heads = lambda a: bf(jnp.pad(a, ((0, 0), (0, MLA_HEADS), (0, 0))))
            att_s = attn_sample(pad_heads(qn_bd), pad_heads(qr_p), rows_sample, cache_t, e, page_table,
                                bf(wuk2.T), bf(wuv2))
            gate_bias = jnp.pad(jnp.concatenate([ml_b_i[e], ml_b_f[e]]), (0, LANES - 2 * ML_HEADS)).reshape(1, LANES)
            ml_conv = (ml_conv_w[e], ml_conv_b[e].reshape(1, ML_W))
            hm_p, c_p, n_p, m_p = mlstm_prompt(z, *ml_conv, bf(ml_w_q[e]), bf(ml_w_k[e]), gate_bias, ml_norm_g[e],
                                               B, TP, Tp)
            hm_s, c_s, n_s, m_s, cb_s = mlstm_sample(zs.reshape(Bs, 1, Z_W), state_mlstm_conv[e], state_mlstm_C[e],
                                                     state_mlstm_n[e], state_mlstm_m[e], *ml_conv,
                                                     ml_w_q[e], ml_w_k[e], gate_bias, ml_norm_g[e])
            x = matmul_residual(jnp.concatenate([att_p, hm_p], axis=1), bf(w_out_even[e]), x)
            xs = matmul_residual(jnp.concatenate([att_s, hm_s], axis=1), w_out_even[e], xs, precise=True)
            x = swiglu_residual(x, norm_ffn_even[e], *ffn_bf, e)
            xs = swiglu_residual(xs, norm_ffn_even[e], *ffn_w, e, precise=True)
            rows_p.append(prompt_view(rows)[:, :Tp])
            rows_s.append(rows_sample.reshape(Bs, 1, LAT_W))
            Cp.append(c_p); np_.append(n_p); mp.append(m_p[:, :, 0])
            Cs.append(c_s); ns.append(n_s); ms.append(m_s)
            cbp.append(prompt_view(z)[:, Tp - (CONV_W - 1):Tp, Z_U:Z_U + ML_W])
            cbs.append(cb_s)
        else:
            o = layer // 2
            W = lru_conv_w.shape[2]
            zz = norm_matmul(x, norm_mix_odd[o], bf(w_in_odd[o]))
            zs = norm_matmul(xs, norm_mix_odd[o], w_in_odd[o], precise=True)
            c = (-LRU_C * jax.nn.softplus(-lru_lam[o])).reshape(1, W)
            lru_conv = (lru_conv_w[o], lru_conv_b[o].reshape(1, W))
            lru_bias = (lru_b_a[o].reshape(1, W), lru_b_x[o].reshape(1, W), c)
            y_p, h_p = rglru_prompt(zz, *lru_conv, bf(lru_w_a[o]), bf(lru_w_x[o]), *lru_bias, B, TP, Tp)
            y_s, h_s = rglru_sample(zs[:, :W], zs[:, W:], state_rglru_conv[o], state_rglru_h[o], *lru_conv,
                                    lru_w_a[o], lru_w_x[o], *lru_bias)
            x, x2d = matmul_residual(y_p, bf(w_out_odd[o]), x, row_tiles=True)
            xs = matmul_residual(y_s, w_out_odd[o], xs, precise=True)
            x = moe_sparse_residual(x, x2d, norm_ffn_odd[o], router(x, norm_ffn_odd[o], moe_router[o]),
                                    *moe_bf, o)
            xs = swiglu_residual(xs, norm_ffn_odd[o], *moe_bf, o, router(xs, norm_ffn_odd[o], moe_router[o]))
            hp_l.append(h_p[:, 0]); hs_l.append(h_s)
            lbp.append(prompt_view(zz)[:, Tp - (CONV_W - 1):Tp, W:])
            lbs.append(jnp.concatenate([state_rglru_conv[o][:, 1:], zs[:, None, W:]], axis=1))
    y_prompt = prompt_view(x)[:, N_META:Tp]
    y_sample = xs.reshape(Bs, 1, D)
    return (y_prompt, y_sample,
            jnp.stack(rows_p), jnp.stack(rows_s),
            jnp.stack(Cp), jnp.stack(Cs), jnp.stack(np_), jnp.stack(ns), jnp.stack(mp), jnp.stack(ms),
            jnp.stack(cbp), jnp.stack(cbs),
            jnp.stack(hp_l), jnp.stack(hs_l), jnp.stack(lbp), jnp.stack(lbs))
```
